```python
import jax, jax.numpy as jnp
from jax import lax
import numpy as np

D_MODEL = 2048
BATCH = 1
SEQ = 16384
DEPTH = 1
DEC_BATCH = 4
DEC_SEQ = 8192
PAST_LEN = 128

HEAD_DIM = 64
ATTN_WIDTH = D_MODEL // 2
GM_WIDTH = D_MODEL - ATTN_WIDTH
N_Q_HEADS = ATTN_WIDTH // HEAD_DIM
N_KV_HEADS = N_Q_HEADS // 4
GQA_GROUP = N_Q_HEADS // N_KV_HEADS
KV_WIDTH = N_KV_HEADS * HEAD_DIM
GM_GROUPS = GM_WIDTH // HEAD_DIM
GM_GROUP_DIM = GM_WIDTH // GM_GROUPS
CHUNK = 128
WINDOW = 128
BLOCK = WINDOW
N_EXPERTS = 16
EXPERT_FF = D_MODEL // 2
CAPACITY_FACTOR = 2
IN_WIDTH = ATTN_WIDTH + 2 * KV_WIDTH + 2 * GM_WIDTH
EPS = 1e-6
NEG_INF = -1e30

kernel_name = "hybrid_window_gqa_spatial_gating_ec_moe_encoder"


def rms_norm(x, g):
    xf = x.astype(jnp.float32)
    y = xf * lax.rsqrt(jnp.mean(xf * xf, axis=-1, keepdims=True) + EPS)
    return (y * g.astype(jnp.float32)).astype(x.dtype)


def alibi_slopes(n):
    return 2.0 ** (-8.0 * jnp.arange(1, n + 1, dtype=jnp.float32) / n)


def banded_attention(q, k, v, sink):
    B, S = q.shape[0], q.shape[1]
    nb = S // BLOCK
    qb = q.reshape(B, nb, BLOCK, N_KV_HEADS, GQA_GROUP, HEAD_DIM)

    def band(t):
        tp = jnp.pad(t, ((0, 0), (BLOCK, BLOCK), (0, 0), (0, 0)))
        tb = tp.reshape(B, nb + 2, BLOCK, N_KV_HEADS, HEAD_DIM)
        return jnp.concatenate([tb[:, :-2], tb[:, 1:-1], tb[:, 2:]], axis=2)

    kb, vb = band(k), band(v)
    scores = jnp.einsum('bnqkgd,bnskd->bnkgqs', qb, kb).astype(jnp.float32) * (HEAD_DIM ** -0.5)

    qi = jnp.arange(BLOCK)[:, None]
    kj = jnp.arange(3 * BLOCK)[None, :]
    rel = qi + BLOCK - kj
    s_abs = jnp.arange(nb)[:, None, None] * BLOCK - BLOCK + kj[None]
    valid = (jnp.abs(rel)[None] <= WINDOW) & (s_abs >= 0) & (s_abs < S)
    slopes = alibi_slopes(N_Q_HEADS).reshape(N_KV_HEADS, GQA_GROUP)
    bias = -slopes[:, :, None, None] * jnp.abs(rel).astype(jnp.float32)[None, None]
    scores = jnp.where(valid[None, :, None, None], scores + bias[None, None], NEG_INF)

    sink_l = sink.astype(jnp.float32).reshape(N_KV_HEADS, GQA_GROUP)[None, None, :, :, None, None]
    m = jnp.maximum(jnp.max(scores, axis=-1, keepdims=True), sink_l)
    p = jnp.exp(scores - m)
    denom = jnp.sum(p, axis=-1, keepdims=True) + jnp.exp(sink_l - m)
    out = jnp.einsum('bnkgqs,bnskd->bnqkgd', (p / denom).astype(v.dtype), vb)
    return out.reshape(B, S, ATTN_WIDTH)


def chunked_spatial_gating(u, gv, gm_norm_g, w_s, b_s):
    B, S = u.shape[0], u.shape[1]
    nc = S // CHUNK
    u = jax.nn.gelu(u)
    gv = rms_norm(jax.nn.gelu(gv).reshape(B, S, GM_GROUPS, GM_GROUP_DIM),
                  gm_norm_g.reshape(GM_GROUPS, GM_GROUP_DIM))
    vc = gv.reshape(B, nc, CHUNK, GM_GROUPS, GM_GROUP_DIM)
    s = jnp.einsum('gts,bnsgc->bntgc', w_s, vc) + b_s.T[None, None, :, :, None]
    return u * s.reshape(B, S, GM_WIDTH)


def expert_choice_moe(x, w_router, w_gate, w_up, w_down):
    B, S, D = x.shape
    T = B * S
    xt = x.reshape(T, D)
    cap = max(1, CAPACITY_FACTOR * T // N_EXPERTS)
    aff = jax.nn.softmax((xt @ w_router).astype(jnp.float32), axis=-1)
    gates, idx = lax.top_k(aff.T, cap)
    xe = xt[idx]
    h = jax.nn.silu(jnp.einsum('ecd,edf->ecf', xe, w_gate)) * jnp.einsum('ecd,edf->ecf', xe, w_up)
    ye = jnp.einsum('ecf,efd->ecd', h, w_down) * gates[..., None].astype(x.dtype)
    y = jnp.zeros_like(xt).at[idx.reshape(-1)].add(ye.reshape(-1, D))
    return y.reshape(B, S, D)


def hybrid_layer(x, attn_norm_g, w_in, q_norm_g, k_norm_g, sink, gm_norm_g, w_spatial, b_spatial,
                 attn_out_g, gm_out_g, w_out, ffn_norm_g, w_router, w_gate, w_up, w_down):
    B, S, _ = x.shape
    h = rms_norm(x, attn_norm_g)
    z = h @ w_in
    splits = [ATTN_WIDTH, ATTN_WIDTH + KV_WIDTH, ATTN_WIDTH + 2 * KV_WIDTH,
              ATTN_WIDTH + 2 * KV_WIDTH + GM_WIDTH]
    q, k, v, u, gv = jnp.split(z, splits, axis=-1)
    q = rms_norm(q.reshape(B, S, N_Q_HEADS, HEAD_DIM), q_norm_g)
    k = rms_norm(k.reshape(B, S, N_KV_HEADS, HEAD_DIM), k_norm_g)
    v = v.reshape(B, S, N_KV_HEADS, HEAD_DIM)
    attn = banded_attention(q, k, v, sink)
    gm = chunked_spatial_gating(u, gv, gm_norm_g, w_spatial, b_spatial)
    mixed = jnp.concatenate([rms_norm(attn, attn_out_g), rms_norm(gm, gm_out_g)], axis=-1) @ w_out
    x = x + mixed
    x = x + expert_choice_moe(rms_norm(x, ffn_norm_g), w_router, w_gate, w_up, w_down)
    return x


def setup_inputs(seed: int = 0) -> dict:
    key = jax.random.key(seed)
    ks = jax.random.split(key, 20)
    f32 = jnp.float32
    nrm = lambda k, shape, s: jax.random.normal(k, shape, f32) * s
    L = DEPTH
    return {
        "x_prompt": jax.random.normal(ks[0], (BATCH, SEQ, D_MODEL), f32),
        "x_sample": jax.random.normal(ks[1], (DEC_BATCH, DEC_SEQ, D_MODEL), f32),
        "attn_norm_g": 1.0 + nrm(ks[2], (L, D_MODEL), 0.02),
        "w_in": nrm(ks[3], (L, D_MODEL, IN_WIDTH), D_MODEL ** -0.5),
        "q_norm_g": 1.0 + nrm(ks[4], (L, HEAD_DIM), 0.02),
        "k_norm_g": 1.0 + nrm(ks[5], (L, HEAD_DIM), 0.02),
        "sink": nrm(ks[6], (L, N_Q_HEADS), 0.5),
        "gm_norm_g": 1.0 + nrm(ks[7], (L, GM_WIDTH), 0.02),
        "w_spatial": nrm(ks[8], (L, GM_GROUPS, CHUNK, CHUNK), CHUNK ** -0.5),
        "b_spatial": 1.0 + nrm(ks[9], (L, GM_GROUPS, CHUNK), 0.02),
        "attn_out_g": 1.0 + nrm(ks[10], (L, ATTN_WIDTH), 0.02),
        "gm_out_g": 1.0 + nrm(ks[11], (L, GM_WIDTH), 0.02),
        "w_out": nrm(ks[12], (L, D_MODEL, D_MODEL), D_MODEL ** -0.5),
        "ffn_norm_g": 1.0 + nrm(ks[13], (L, D_MODEL), 0.02),
        "w_router": nrm(ks[14], (L, D_MODEL, N_EXPERTS), D_MODEL ** -0.5),
        "w_gate": nrm(ks[15], (L, N_EXPERTS, D_MODEL, EXPERT_FF), D_MODEL ** -0.5),
        "w_up": nrm(ks[16], (L, N_EXPERTS, D_MODEL, EXPERT_FF), D_MODEL ** -0.5),
        "w_down": nrm(ks[17], (L, N_EXPERTS, EXPERT_FF, D_MODEL), EXPERT_FF ** -0.5),
    }


def reference(x_prompt, x_sample, attn_norm_g, w_in, q_norm_g, k_norm_g, sink, gm_norm_g, w_spatial,
              b_spatial, attn_out_g, gm_out_g, w_out, ffn_norm_g, w_router, w_gate, w_up, w_down):
    y_prompt = x_prompt
    y_sample = x_sample
    for l in range(DEPTH):
        params = (attn_norm_g[l], w_in[l], q_norm_g[l], k_norm_g[l], sink[l], gm_norm_g[l],
                  w_spatial[l], b_spatial[l], attn_out_g[l], gm_out_g[l], w_out[l], ffn_norm_g[l],
                  w_router[l], w_gate[l], w_up[l], w_down[l])
        y_prompt = hybrid_layer(y_prompt, *params)
        y_sample = hybrid_layer(y_sample, *params)
    return (y_prompt, y_sample)
```

```python
import functools

import jax
import jax.numpy as jnp
from jax import lax
from jax.experimental import pallas as pl
from jax.experimental.pallas import tpu as pltpu

F32 = jnp.float32
BF16 = jnp.bfloat16
I32 = jnp.int32

D_MODEL = 2048
HEAD_DIM = 64
ATTN_WIDTH = 1024
KV_WIDTH = 256
GM_WIDTH = 1024
N_Q_HEADS = 16
N_KV_HEADS = 4
GQA_GROUP = 4
GM_GROUPS = 16
CHUNK = 128
N_EXPERTS = 16
EXPERT_FF = 1024
CAPACITY_FACTOR = 2
EPS = 1e-6
NEG_INF = -1e30

LANES = 128
VMEM_LIMIT = 58 * 1024 * 1024

IN_TOKENS = 512
MIX_TOKENS = 512
FFN_ROWS = 512
COMB_TOKENS = 256
COMB_PAIRS = 256

_NT = (((1,), (1,)), ((), ()))


def _const_spec(shape):
    nd = len(shape)
    return pl.BlockSpec(shape, lambda *_: (0,) * nd, pipeline_mode=pl.Buffered(1))


def _split3(a):
    a1 = a.astype(BF16)
    r1 = a - a1.astype(F32)
    a2 = r1.astype(BF16)
    a3 = (r1 - a2.astype(F32)).astype(BF16)
    return a1, a2, a3


def _exact_dot(a, b01):
    a1, a2, a3 = _split3(a)
    d = lambda x: jnp.dot(x, b01, preferred_element_type=F32)
    return (d(a1) + d(a2)) + d(a3)


def _exact_dot_left(a01, b):
    b1, b2, b3 = _split3(b)
    d = lambda x: jnp.dot(a01, x, preferred_element_type=F32)
    return (d(b1) + d(b2)) + d(b3)


def _gelu(x):
    return jax.nn.gelu(x)


def _in_proj_kernel(x_ref, ang_ref, wqkv_ref, wugt_ref, qg_ref, kg_ref, gng_ref,
                    q_ref, k_ref, v_ref, ut_ref, gt_ref):
    xf = x_ref[...]
    ms = jnp.mean(xf * xf, axis=-1, keepdims=True)
    h = (xf * lax.rsqrt(ms + EPS) * ang_ref[...]).astype(BF16)
    z = jnp.dot(h, wqkv_ref[...], preferred_element_type=F32)

    low = lax.broadcasted_iota(I32, (1, LANES), 1) < HEAD_DIM

    def two_head_norm(zt, g):
        sq = zt * zt
        s_lo = jnp.sum(jnp.where(low, sq, 0.0), axis=-1, keepdims=True)
        s_hi = jnp.sum(jnp.where(low, 0.0, sq), axis=-1, keepdims=True)
        r = jnp.where(low, lax.rsqrt(s_lo * (1.0 / HEAD_DIM) + EPS), lax.rsqrt(s_hi * (1.0 / HEAD_DIM) + EPS))
        return zt * r * g

    for j in range(ATTN_WIDTH // LANES):
        q_ref[:, j * LANES:(j + 1) * LANES] = two_head_norm(z[:, j * LANES:(j + 1) * LANES], qg_ref[...]).astype(BF16)
    for j in range(KV_WIDTH // LANES):
        c0 = ATTN_WIDTH + j * LANES
        k_ref[:, j * LANES:(j + 1) * LANES] = two_head_norm(z[:, c0:c0 + LANES], kg_ref[...]).astype(BF16)
    v_ref[...] = z[:, ATTN_WIDTH + KV_WIDTH:].astype(BF16)

    zt = lax.dot_general(wugt_ref[...], h, _NT, preferred_element_type=F32)
    ut_ref[...] = _gelu(zt[:GM_WIDTH])
    gl = _gelu(zt[GM_WIDTH:])
    tm = gl.shape[1]
    gl3 = gl.reshape(GM_GROUPS, HEAD_DIM, tm)
    msg = jnp.mean(gl3 * gl3, axis=1, keepdims=True)
    gn = gl3 * lax.rsqrt(msg + EPS) * gng_ref[...]
    gt_ref[...] = gn.reshape(GM_WIDTH, tm).astype(BF16)


def _in_proj(x2d, ang, wqkv, wugt, qg, kg, gng):
    t = x2d.shape[0]
    tm = IN_TOKENS
    assert t % tm == 0
    row = lambda i: (i, 0)
    col = lambda i: (0, i)
    return pl.pallas_call(
        _in_proj_kernel,
        grid=(t // tm,),
        in_specs=[
            pl.BlockSpec((tm, D_MODEL), row),
            _const_spec((1, D_MODEL)),
            _const_spec((D_MODEL, ATTN_WIDTH + 2 * KV_WIDTH)),
            _const_spec((2 * GM_WIDTH, D_MODEL)),
            _const_spec((1, LANES)),
            _const_spec((1, LANES)),
            _const_spec((GM_GROUPS, HEAD_DIM, 1)),
        ],
        out_specs=[
            pl.BlockSpec((tm, ATTN_WIDTH), row),
            pl.BlockSpec((tm, KV_WIDTH), row),
            pl.BlockSpec((tm, KV_WIDTH), row),
            pl.BlockSpec((GM_WIDTH, tm), col),
            pl.BlockSpec((GM_WIDTH, tm), col),
        ],
        out_shape=[
            jax.ShapeDtypeStruct((t, ATTN_WIDTH), BF16),
            jax.ShapeDtypeStruct((t, KV_WIDTH), BF16),
            jax.ShapeDtypeStruct((t, KV_WIDTH), BF16),
            jax.ShapeDtypeStruct((GM_WIDTH, t), F32),
            jax.ShapeDtypeStruct((GM_WIDTH, t), BF16),
        ],
        compiler_params=pltpu.CompilerParams(
            dimension_semantics=("arbitrary",), vmem_limit_bytes=VMEM_LIMIT),
        name="in_proj",
    )(x2d, ang, wqkv, wugt, qg, kg, gng)


def _mixer_kernel(q_ref, kp_ref, kc_ref, kn_ref, vp_ref, vc_ref, vn_ref, ut_ref, gt_ref, x_ref,
                  bias_ref, sink_ref, wsp_ref, bsp_ref, aog_ref, gog_ref, wout_ref, fng_ref,
                  wrs_ref, wrh_ref, x1_ref, aff_ref, *, blocks_per_seq):
    i = pl.program_id(0)
    tb = q_ref.shape[0]
    nsub = tb // CHUNK
    first = (i % blocks_per_seq) == 0
    last = (i % blocks_per_seq) == blocks_per_seq - 1

    kcat = jnp.concatenate([kp_ref[...], kc_ref[...], kn_ref[...]], axis=0)
    vcat = jnp.concatenate([vp_ref[...], vc_ref[...], vn_ref[...]], axis=0)
    key = lax.broadcasted_iota(I32, (1, 3 * CHUNK), 1)

    attn_rows = []
    for sb in range(nsub):
        edge = jnp.zeros((1, 3 * CHUNK), F32)
        if sb == 0:
            edge = jnp.where((key < CHUNK) & first, NEG_INF, edge)
        if sb == nsub - 1:
            edge = jnp.where((key >= 2 * CHUNK) & last, NEG_INF, edge)
        kb = kcat[sb * CHUNK:sb * CHUNK + 3 * CHUNK]
        vb = vcat[sb * CHUNK:sb * CHUNK + 3 * CHUNK]
        qs = q_ref[sb * CHUNK:(sb + 1) * CHUNK, :]
        head_out = []
        for j in range(N_KV_HEADS):
            qg = jnp.concatenate(
                [qs[:, (GQA_GROUP * j + g) * HEAD_DIM:(GQA_GROUP * j + g + 1) * HEAD_DIM] for g in range(GQA_GROUP)],
                axis=0)
            s = lax.dot_general(qg, kb[:, j * HEAD_DIM:(j + 1) * HEAD_DIM], _NT,
                                preferred_element_type=F32) * (HEAD_DIM ** -0.5)
            ps, dens = [], []
            for g in range(GQA_GROUP):
                hd = GQA_GROUP * j + g
                sh = s[g * CHUNK:(g + 1) * CHUNK] + bias_ref[hd] + edge
                snk = sink_ref[hd]
                m = jnp.maximum(jnp.max(sh, axis=-1, keepdims=True), snk)
                p = jnp.exp(sh - m)
                dens.append(jnp.sum(p, axis=-1, keepdims=True) + jnp.exp(snk - m))
                ps.append(p.astype(BF16))
            o = jnp.dot(jnp.concatenate(ps, axis=0), vb[:, j * HEAD_DIM:(j + 1) * HEAD_DIM],
                        preferred_element_type=F32)
            for g in range(GQA_GROUP):
                head_out.append(o[g * CHUNK:(g + 1) * CHUNK] / dens[g])
        attn_rows.append(jnp.concatenate(head_out, axis=1))
    attn = jnp.concatenate(attn_rows, axis=0)
    an = attn * lax.rsqrt(jnp.mean(attn * attn, axis=-1, keepdims=True) + EPS) * aog_ref[...]

    st_rows = []
    for pi in range(GM_GROUPS // 2):
        r0 = pi * 2 * HEAD_DIM
        lhs = jnp.concatenate(
            [jnp.concatenate([gt_ref[r0:r0 + HEAD_DIM, c * CHUNK:(c + 1) * CHUNK],
                              gt_ref[r0 + HEAD_DIM:r0 + 2 * HEAD_DIM, c * CHUNK:(c + 1) * CHUNK]], axis=1)
             for c in range(nsub)], axis=0)
        o = jnp.dot(lhs, wsp_ref[pi], preferred_element_type=F32)
        ga = jnp.concatenate([o[c * HEAD_DIM:(c + 1) * HEAD_DIM, :CHUNK] for c in range(nsub)], axis=1)
        gb = jnp.concatenate([o[c * HEAD_DIM:(c + 1) * HEAD_DIM, CHUNK:] for c in range(nsub)], axis=1)
        ba = jnp.concatenate([bsp_ref[2 * pi]] * nsub, axis=1)
        bb = jnp.concatenate([bsp_ref[2 * pi + 1]] * nsub, axis=1)
        st_rows.append(ga + ba)
        st_rows.append(gb + bb)
    gmt = ut_ref[...] * jnp.concatenate(st_rows, axis=0)
    gm = gmt.T
    gn = gm * lax.rsqrt(jnp.mean(gm * gm, axis=-1, keepdims=True) + EPS) * gog_ref[...]

    cat = jnp.concatenate([an.astype(BF16), gn.astype(BF16)], axis=1)
    x1 = x_ref[...] + jnp.dot(cat, wout_ref[...], preferred_element_type=F32)
    x1_ref[...] = x1

    xn = x1 * lax.rsqrt(jnp.mean(x1 * x1, axis=-1, keepdims=True) + EPS) * fng_ref[...]
    xh = xn.astype(BF16)
    xl = (xn - xh.astype(F32)).astype(BF16)
    r1 = lax.dot_general(wrs_ref[...], xh, _NT, preferred_element_type=F32)
    r2 = lax.dot_general(wrh_ref[...], xl, _NT, preferred_element_type=F32)
    logits = (r1[:N_EXPERTS] + r1[N_EXPERTS:]) + r2
    mx = jnp.max(logits, axis=0, keepdims=True)
    ex = jnp.exp(logits - mx)
    aff_ref[...] = ex / jnp.sum(ex, axis=0, keepdims=True)


def _mixer(q, k, v, ut, gt, x2d, seq_len, consts):
    t = q.shape[0]
    tb = MIX_TOKENS
    assert seq_len % tb == 0 and t % seq_len == 0
    r = tb // CHUNK
    nch = t // CHUNK
    row = lambda i: (i, 0)
    col = lambda i: (0, i)
    prev = lambda i: (jnp.maximum(i * r - 1, 0), 0)
    nxt = lambda i: (jnp.minimum((i + 1) * r, nch - 1), 0)
    (bias, sink, wsp, bsp, aog, gog, wout, fng, wrs, wrh) = consts
    kernel = functools.partial(_mixer_kernel, blocks_per_seq=seq_len // tb)
    return pl.pallas_call(
        kernel,
        grid=(t // tb,),
        in_specs=[
            pl.BlockSpec((tb, ATTN_WIDTH), row),
            pl.BlockSpec((CHUNK, KV_WIDTH), prev),
            pl.BlockSpec((tb, KV_WIDTH), row),
            pl.BlockSpec((CHUNK, KV_WIDTH), nxt),
            pl.BlockSpec((CHUNK, KV_WIDTH), prev),
            pl.BlockSpec((tb, KV_WIDTH), row),
            pl.BlockSpec((CHUNK, KV_WIDTH), nxt),
            pl.BlockSpec((GM_WIDTH, tb), col),
            pl.BlockSpec((GM_WIDTH, tb), col),
            pl.BlockSpec((tb, D_MODEL), row),
            _const_spec((N_Q_HEADS, CHUNK, 3 * CHUNK)),
            pl.BlockSpec(memory_space=pltpu.SMEM),
            _const_spec((GM_GROUPS // 2, 2 * CHUNK, 2 * CHUNK)),
            _const_spec((GM_GROUPS, 1, CHUNK)),
            _const_spec((1, ATTN_WIDTH)),
            _const_spec((1, GM_WIDTH)),
            _const_spec((D_MODEL, D_MODEL)),
            _const_spec((1, D_MODEL)),
            _const_spec((2 * N_EXPERTS, D_MODEL)),
            _const_spec((N_EXPERTS, D_MODEL)),
        ],
        out_specs=[
            pl.BlockSpec((tb, D_MODEL), row),
            pl.BlockSpec((N_EXPERTS, tb), col),
        ],
        out_shape=[
            jax.ShapeDtypeStruct((t, D_MODEL), F32),
            jax.ShapeDtypeStruct((N_EXPERTS, t), F32),
        ],
        compiler_params=pltpu.CompilerParams(
            dimension_semantics=("arbitrary",), vmem_limit_bytes=VMEM_LIMIT),
        name="mixer",
    )(q, k, k, k, v, v, v, ut, gt, x2d, bias, sink, wsp, bsp, aog, gog, wout, fng, wrs, wrh)


def _sum12(x):
    return jnp.sum(jnp.sum(x, axis=2, keepdims=True), axis=1, keepdims=True)


def _select_kernel(a_ref, at_ref, m_ref, mt_ref, ppt_ref, ctok_ref, cnt_ref, *, cap, tok_bits):
    ne, nb, _ = a_ref.shape
    keys = pltpu.bitcast(a_ref[...], I32)
    capf = jnp.float32(cap)

    def thr_step(it, thr):
        cand = thr | jnp.left_shift(jnp.int32(1), 30 - it)
        cnt = _sum12(jnp.where(keys >= cand, 1.0, 0.0))
        return jnp.where(cnt >= capf, cand, thr)

    thr = lax.fori_loop(0, 31, thr_step, jnp.zeros((ne, 1, 1), I32))

    gt = keys > thr
    eq = keys == thr
    need = capf - _sum12(jnp.where(gt, 1.0, 0.0))
    tok = lax.broadcasted_iota(I32, (1, nb, LANES), 1) * LANES + lax.broadcasted_iota(I32, (1, nb, LANES), 2)

    def cut_step(it, c):
        cand = c + jnp.left_shift(jnp.int32(1), tok_bits - 1 - it)
        f = _sum12(jnp.where(eq & (tok < cand), 1.0, 0.0))
        return jnp.where(f < need, cand, c)

    c = lax.fori_loop(0, tok_bits, cut_step, jnp.zeros((ne, 1, 1), I32))
    tcut = jnp.where(need > 0.0, c + 1, 0)

    m = jnp.where(gt | (eq & (tok < tcut)), 1.0, 0.0)
    m_ref[...] = m.astype(BF16)

    keys_t = pltpu.bitcast(at_ref[...], I32)
    tok_t = lax.broadcasted_iota(I32, (1, LANES, nb), 2) * LANES + lax.broadcasted_iota(I32, (1, LANES, nb), 1)
    mt = jnp.where((keys_t > thr) | ((keys_t == thr) & (tok_t < tcut)), 1.0, 0.0)
    mt_ref[...] = mt.astype(BF16)

    cnt_t = jnp.sum(mt, axis=0)
    rl = lax.broadcasted_iota(I32, (LANES, LANES), 0)
    cl = lax.broadcasted_iota(I32, (LANES, LANES), 1)
    tri_strict_low = jnp.where(cl < rl, 1.0, 0.0).astype(BF16)
    local_t = jnp.dot(tri_strict_low, cnt_t.astype(BF16), preferred_element_type=F32)
    tot_t = jnp.sum(cnt_t, axis=0, keepdims=True)
    rb = lax.broadcasted_iota(I32, (nb, nb), 0)
    cb = lax.broadcasted_iota(I32, (nb, nb), 1)
    blk_before = jnp.where(rb < cb, 1.0, 0.0).astype(BF16)
    pb_t = _exact_dot(jnp.broadcast_to(tot_t, (8, nb)), blk_before)[0:1]
    ctok_t = local_t + pb_t
    rank = jnp.zeros((LANES, nb), F32)
    for e in range(ne):
        ppt_ref[e] = ctok_t + rank
        rank = rank + mt[e]

    cnt_b = jnp.sum(m, axis=0)
    tri_strict_up = jnp.where(rl < cl, 1.0, 0.0).astype(BF16)
    local_b = jnp.dot(cnt_b.astype(BF16), tri_strict_up, preferred_element_type=F32)
    totb = jnp.dot(cnt_b.astype(BF16), jnp.ones((LANES, LANES), BF16), preferred_element_type=F32)
    blk_after = jnp.where(cb < rb, 1.0, 0.0).astype(BF16)
    ctok_ref[...] = local_b + _exact_dot_left(blk_after, totb)
    cnt_ref[...] = cnt_b


def _select(a_blk, a_t, cap, tok_bits):
    ne, nb, _ = a_blk.shape
    kernel = functools.partial(_select_kernel, cap=cap, tok_bits=tok_bits)
    full3 = lambda s: pl.BlockSpec(s, lambda: (0, 0, 0))
    full2 = lambda s: pl.BlockSpec(s, lambda: (0, 0))
    return pl.pallas_call(
        kernel,
        in_specs=[full3((ne, nb, LANES)), full3((ne, LANES, nb))],
        out_specs=[full3((ne, nb, LANES)), full3((ne, LANES, nb)), full3((ne, LANES, nb)),
                   full2((nb, LANES)), full2((nb, LANES))],
        out_shape=[
            jax.ShapeDtypeStruct((ne, nb, LANES), BF16),
            jax.ShapeDtypeStruct((ne, LANES, nb), BF16),
            jax.ShapeDtypeStruct((ne, LANES, nb), F32),
            jax.ShapeDtypeStruct((nb, LANES), F32),
            jax.ShapeDtypeStruct((nb, LANES), F32),
        ],
        compiler_params=pltpu.CompilerParams(vmem_limit_bytes=VMEM_LIMIT),
        name="select",
    )(a_blk, a_t)


def _compact_kernel(m_ref, mt_ref, at_ref, ppt_ref, idx_ref, gate_ref, pp_ref):
    nb = m_ref.shape[1]
    nchunks = idx_ref.shape[1]
    mb = m_ref[0]
    mt = mt_ref[0]
    a_t = at_ref[0]
    pp_t = ppt_ref[0]

    totb = jnp.dot(mb, jnp.ones((LANES, LANES), BF16), preferred_element_type=F32)
    rb = lax.broadcasted_iota(I32, (nb, nb), 0)
    cb = lax.broadcasted_iota(I32, (nb, nb), 1)
    blk_after = jnp.where(cb < rb, 1.0, 0.0).astype(BF16)
    cb_excl = jnp.dot(blk_after, totb.astype(BF16), preferred_element_type=F32)
    cb_incl = cb_excl + totb
    rl = lax.broadcasted_iota(I32, (LANES, LANES), 0)
    cl = lax.broadcasted_iota(I32, (LANES, LANES), 1)
    tri_low = jnp.where(cl <= rl, 1.0, 0.0).astype(BF16)
    lt = jnp.dot(tri_low, mt, preferred_element_type=F32).astype(BF16)
    sub_b = lax.broadcasted_iota(I32, (nb, LANES), 0).astype(F32)
    sub_l = rl.astype(F32)
    lane = lax.broadcasted_iota(I32, (1, LANES), 1).astype(F32)

    def chunk(c, carry):
        j = lane + lax.convert_element_type(c * LANES, F32)
        bj = jnp.sum(jnp.where(cb_incl <= j, 1.0, 0.0), axis=0, keepdims=True)
        oh = sub_b == bj
        ohb = jnp.where(oh, 1.0, 0.0).astype(BF16)
        jloc = j - jnp.sum(jnp.where(oh, cb_excl, 0.0), axis=0, keepdims=True)
        glt = jnp.dot(lt, ohb, preferred_element_type=F32)
        lj = jnp.sum(jnp.where(glt <= jloc, 1.0, 0.0), axis=0, keepdims=True)
        ohl = sub_l == lj
        gate = jnp.sum(jnp.where(ohl, _exact_dot(a_t, ohb), 0.0), axis=0, keepdims=True)
        pp = jnp.sum(jnp.where(ohl, _exact_dot(pp_t, ohb), 0.0), axis=0, keepdims=True)
        idx_ref[0, pl.ds(c, 1), :] = (bj * LANES + lj).astype(I32)
        gate_ref[0, pl.ds(c, 1), :] = gate
        pp_ref[0, pl.ds(c, 1), :] = pp.astype(I32)
        return carry

    lax.fori_loop(0, nchunks, chunk, 0)


def _compact(m_blk, m_t, a_t, pp_t, cap):
    ne, nb, _ = m_blk.shape
    assert cap % LANES == 0
    nchunks = cap // LANES
    per_e = lambda s: pl.BlockSpec((1,) + s, lambda e: (e, 0, 0))
    return pl.pallas_call(
        _compact_kernel,
        grid=(ne,),
        in_specs=[per_e((nb, LANES)), per_e((LANES, nb)), per_e((LANES, nb)), per_e((LANES, nb))],
        out_specs=[per_e((nchunks, LANES))] * 3,
        out_shape=[
            jax.ShapeDtypeStruct((ne, nchunks, LANES), I32),
            jax.ShapeDtypeStruct((ne, nchunks, LANES), F32),
            jax.ShapeDtypeStruct((ne, nchunks, LANES), I32),
        ],
        compiler_params=pltpu.CompilerParams(
            dimension_semantics=("arbitrary",), vmem_limit_bytes=VMEM_LIMIT),
        name="compact",
    )(m_blk, m_t, a_t, pp_t)


def _ffn_kernel(idx_ref, idxn_ref, pp_ref, gate_ref, x_hbm, fng_ref, wg_ref, wu_ref, wd_ref,
                g_hbm, xbuf, ybuf, gsem, ssem, *, tiles_per_expert):
    tm = xbuf.shape[1]
    step = pl.program_id(0) * tiles_per_expert + pl.program_id(1)
    nsteps = pl.num_programs(0) * tiles_per_expert
    slot = step % 2

    def gather_copy(row, r, s):
        return pltpu.make_async_copy(x_hbm.at[pl.ds(row, 1), :], xbuf.at[s, pl.ds(r, 1), :], gsem.at[s])

    def start_gather(iref, s):
        def body(r, carry):
            gather_copy(iref[0, 0, r], r, s).start()
            return carry
        lax.fori_loop(0, tm, body, 0, unroll=8)

    def scatter_copy(row, r):
        return pltpu.make_async_copy(ybuf.at[pl.ds(r, 1), :], g_hbm.at[pl.ds(row, 1), :], ssem.at[0])

    def wait_scatter():
        pltpu.make_async_copy(ybuf, g_hbm.at[pl.ds(0, tm), :], ssem.at[0]).wait()

    @pl.when(step == 0)
    def _():
        start_gather(idx_ref, 0)

    pltpu.make_async_copy(x_hbm.at[pl.ds(0, tm), :], xbuf.at[slot], gsem.at[slot]).wait()

    @pl.when(step + 1 < nsteps)
    def _():
        start_gather(idxn_ref, 1 - slot)

    xt = xbuf[slot]
    xn = (xt * lax.rsqrt(jnp.mean(xt * xt, axis=-1, keepdims=True) + EPS) * fng_ref[...]).astype(BF16)

    ri = lax.broadcasted_iota(I32, (tm, tm), 0)
    ci = lax.broadcasted_iota(I32, (tm, tm), 1)
    eye = jnp.where(ri == ci, 1.0, 0.0).astype(BF16)
    g1, g2, g3 = _split3(jnp.broadcast_to(gate_ref[0], (8, tm)))
    nt = lambda g: lax.dot_general(eye, g, _NT, preferred_element_type=F32)
    gcol = ((nt(g1) + nt(g2)) + nt(g3))[:, 0:1]

    half = EXPERT_FF // 2
    acc = None
    for c in range(2):
        hg = jnp.dot(xn, wg_ref[0, :, c * half:(c + 1) * half], preferred_element_type=F32)
        hu = jnp.dot(xn, wu_ref[0, :, c * half:(c + 1) * half], preferred_element_type=F32)
        hh = (jax.nn.silu(hg) * hu).astype(BF16)
        part = jnp.dot(hh, wd_ref[0, c * half:(c + 1) * half, :], preferred_element_type=F32)
        acc = part if acc is None else acc + part

    @pl.when(step > 0)
    def _():
        wait_scatter()

    ybuf[...] = acc * gcol

    def sbody(r, carry):
        scatter_copy(pp_ref[0, 0, r], r).start()
        return carry
    lax.fori_loop(0, tm, sbody, 0, unroll=8)

    @pl.when(step + 1 == nsteps)
    def _():
        wait_scatter()


def _ffn(idx_t, gate_t, pp_t, x1, fng, wg, wu, wd, n_pairs):
    ntiles, _, tm = idx_t.shape
    tpe = ntiles // N_EXPERTS
    cur = lambda e, i: (e * tpe + i, 0, 0)
    nxt = lambda e, i: (jnp.minimum(e * tpe + i + 1, ntiles - 1), 0, 0)
    wmap = lambda e, i: (e, 0, 0)
    kernel = functools.partial(_ffn_kernel, tiles_per_expert=tpe)
    return pl.pallas_call(
        kernel,
        grid=(N_EXPERTS, tpe),
        in_specs=[
            pl.BlockSpec((1, 1, tm), cur, memory_space=pltpu.SMEM),
            pl.BlockSpec((1, 1, tm), nxt, memory_space=pltpu.SMEM),
            pl.BlockSpec((1, 1, tm), cur, memory_space=pltpu.SMEM),
            pl.BlockSpec((1, 1, tm), cur),
            pl.BlockSpec(memory_space=pl.ANY),
            _const_spec((1, D_MODEL)),
            pl.BlockSpec((1, D_MODEL, EXPERT_FF), wmap),
            pl.BlockSpec((1, D_MODEL, EXPERT_FF), wmap),
            pl.BlockSpec((1, EXPERT_FF, D_MODEL), wmap),
        ],
        out_specs=pl.BlockSpec(memory_space=pl.ANY),
        out_shape=jax.ShapeDtypeStruct((n_pairs, D_MODEL), F32),
        scratch_shapes=[
            pltpu.VMEM((2, tm, D_MODEL), F32),
            pltpu.VMEM((tm, D_MODEL), F32),
            pltpu.SemaphoreType.DMA((2,)),
            pltpu.SemaphoreType.DMA((1,)),
        ],
        compiler_params=pltpu.CompilerParams(
            dimension_semantics=("arbitrary", "arbitrary"), vmem_limit_bytes=VMEM_LIMIT),
        name="ffn",
    )(idx_t, idx_t, pp_t, gate_t, x1, fng, wg, wu, wd)


def _combine_kernel(boff_ref, x1_ref, ctok_ref, cnt_ref, g_hbm, y_ref, gbuf, acc_ref, sem, *, n_pairs):
    i = pl.program_id(0)
    tb = x1_ref.shape[0]
    pc = gbuf.shape[1]
    r = tb // LANES
    p_lo = boff_ref[i]
    p_hi = boff_ref[i + 1]
    start = (p_lo // 8) * 8
    nch = (p_hi - start + pc - 1) // pc

    ri = lax.broadcasted_iota(I32, (LANES, LANES), 0)
    ci = lax.broadcasted_iota(I32, (LANES, LANES), 1)
    eye = jnp.where(ri == ci, 1.0, 0.0).astype(BF16)

    def to_col(rowv):
        p1, p2, p3 = _split3(jnp.broadcast_to(rowv, (8, LANES)))
        nt = lambda p: lax.dot_general(eye, p, _NT, preferred_element_type=F32)
        return ((nt(p1) + nt(p2)) + nt(p3))[:, 0:1]

    cex_rows = ctok_ref[pl.ds(i * r, r), :]
    cnt_rows = cnt_ref[pl.ds(i * r, r), :]
    cex = jnp.concatenate([to_col(cex_rows[k:k + 1]) for k in range(r)], axis=0)
    cin = cex + jnp.concatenate([to_col(cnt_rows[k:k + 1]) for k in range(r)], axis=0)

    def chunk_copy(c, s):
        base = jnp.minimum(start + c * pc, n_pairs - pc)
        return pltpu.make_async_copy(g_hbm.at[pl.ds(base, pc), :], gbuf.at[s], sem.at[s])

    acc_ref[...] = jnp.zeros_like(acc_ref)

    @pl.when(nch > 0)
    def _():
        chunk_copy(0, 0).start()

    def body(c, carry):
        s = c % 2
        chunk_copy(c, s).wait()

        @pl.when(c + 1 < nch)
        def _():
            chunk_copy(c + 1, 1 - s).start()

        want = start + c * pc
        base = jnp.minimum(want, n_pairs - pc)
        p = (base + lax.broadcasted_iota(I32, (1, pc), 1)).astype(F32)
        sel = (cex <= p) & (p < cin) & (p >= lax.convert_element_type(want, F32))
        acc_ref[...] += jnp.dot(jnp.where(sel, 1.0, 0.0).astype(BF16), gbuf[s].astype(BF16),
                                preferred_element_type=F32)
        return carry

    lax.fori_loop(0, nch, body, 0)
    y_ref[...] = x1_ref[...] + acc_ref[...]


def _combine(boff, x1, ctok, cnt, g):
    t = x1.shape[0]
    tb = COMB_TOKENS
    n_pairs = g.shape[0]
    nb = ctok.shape[0]
    kernel = functools.partial(_combine_kernel, n_pairs=n_pairs)
    grid_spec = pltpu.PrefetchScalarGridSpec(
        num_scalar_prefetch=1,
        grid=(t // tb,),
        in_specs=[
            pl.BlockSpec((tb, D_MODEL), lambda i, b: (i, 0)),
            pl.BlockSpec((nb, LANES), lambda i, b: (0, 0)),
            pl.BlockSpec((nb, LANES), lambda i, b: (0, 0)),
            pl.BlockSpec(memory_space=pl.ANY),
        ],
        out_specs=pl.BlockSpec((tb, D_MODEL), lambda i, b: (i, 0)),
        scratch_shapes=[
            pltpu.VMEM((2, COMB_PAIRS, D_MODEL), F32),
            pltpu.VMEM((tb, D_MODEL), F32),
            pltpu.SemaphoreType.DMA((2,)),
        ],
    )
    return pl.pallas_call(
        kernel,
        grid_spec=grid_spec,
        out_shape=jax.ShapeDtypeStruct((t, D_MODEL), F32),
        compiler_params=pltpu.CompilerParams(
            dimension_semantics=("arbitrary",), vmem_limit_bytes=VMEM_LIMIT),
        name="combine",
    )(boff, x1, ctok, cnt, g)


def _alibi_bias():
    slopes = 2.0 ** (-8.0 * jnp.arange(1, N_Q_HEADS + 1, dtype=F32) / N_Q_HEADS)
    qi = jnp.arange(CHUNK)[:, None]
    kj = jnp.arange(3 * CHUNK)[None, :]
    rel = jnp.abs(qi + CHUNK - kj)
    bias = -slopes[:, None, None] * rel.astype(F32)[None]
    return jnp.where((rel <= CHUNK)[None], bias, NEG_INF)


def _prep_weights(attn_norm_g, w_in, q_norm_g, k_norm_g, sink, gm_norm_g, w_spatial, b_spatial,
                  attn_out_g, gm_out_g, w_out, ffn_norm_g, w_router, w_gate, w_up, w_down):
    nqkv = ATTN_WIDTH + 2 * KV_WIDTH
    ang = attn_norm_g.reshape(1, D_MODEL)
    wqkv = w_in[:, :nqkv].astype(BF16)
    wugt = w_in[:, nqkv:].T.astype(BF16)
    qg = jnp.tile(q_norm_g, 2).reshape(1, LANES)
    kg = jnp.tile(k_norm_g, 2).reshape(1, LANES)
    gng = gm_norm_g.reshape(GM_GROUPS, HEAD_DIM, 1)
    wst = jnp.swapaxes(w_spatial, 1, 2).astype(BF16)
    zero = jnp.zeros((GM_GROUPS // 2, CHUNK, CHUNK), BF16)
    wsp = jnp.concatenate([jnp.concatenate([wst[0::2], zero], axis=2),
                           jnp.concatenate([zero, wst[1::2]], axis=2)], axis=1)
    bsp = b_spatial.reshape(GM_GROUPS, 1, CHUNK)
    aog = attn_out_g.reshape(1, ATTN_WIDTH)
    gog = gm_out_g.reshape(1, GM_WIDTH)
    wout = w_out.astype(BF16)
    fng = ffn_norm_g.reshape(1, D_MODEL)
    wr_t = w_router.T
    wr_hi = wr_t.astype(BF16)
    wr_lo = (wr_t - wr_hi.astype(F32)).astype(BF16)
    wrs = jnp.concatenate([wr_hi, wr_lo], axis=0)
    mixer_consts = (_alibi_bias(), sink, wsp, bsp, aog, gog, wout, fng, wrs, wr_hi)
    in_consts = (ang, wqkv, wugt, qg, kg, gng)
    ffn_consts = (fng, w_gate.astype(BF16), w_up.astype(BF16), w_down.astype(BF16))
    return in_consts, mixer_consts, ffn_consts


def _layer(x, in_consts, mixer_consts, ffn_consts):
    b, s, d = x.shape
    t = b * s
    x2d = x.reshape(t, d)
    q, k, v, ut, gt = _in_proj(x2d, *in_consts)
    x1, aff_t = _mixer(q, k, v, ut, gt, x2d, s, mixer_consts)

    nb = t // LANES
    cap = max(1, CAPACITY_FACTOR * t // N_EXPERTS)
    tok_bits = max(1, (t - 1).bit_length())
    a_blk = aff_t.reshape(N_EXPERTS, nb, LANES)
    a_t = jnp.swapaxes(a_blk, 1, 2)
    m_blk, m_t, pp_t, ctok, cnt = _select(a_blk, a_t, cap, tok_bits)
    idx, gate, pp = _compact(m_blk, m_t, a_t, pp_t, cap)

    tm = min(FFN_ROWS, cap)
    ntiles = N_EXPERTS * cap // tm
    n_pairs = N_EXPERTS * cap
    g = _ffn(idx.reshape(ntiles, 1, tm), gate.reshape(ntiles, 1, tm), pp.reshape(ntiles, 1, tm),
             x1, *ffn_consts, n_pairs)

    boff = jnp.concatenate([ctok.reshape(-1)[::COMB_TOKENS].astype(I32), jnp.full((1,), n_pairs, I32)])
    y = _combine(boff, x1, ctok, cnt, g)
    return y.reshape(b, s, d)


def kernel(x_prompt, x_sample, attn_norm_g, w_in, q_norm_g, k_norm_g, sink, gm_norm_g, w_spatial, b_spatial, attn_out_g, gm_out_g, w_out, ffn_norm_g, w_router, w_gate, w_up, w_down):
    y_prompt, y_sample = x_prompt, x_sample
    for l in range(w_in.shape[0]):
        consts = _prep_weights(attn_norm_g[l], w_in[l], q_norm_g[l], k_norm_g[l], sink[l], gm_norm_g[l],
                               w_spatial[l], b_spatial[l], attn_out_g[l], gm_out_g[l], w_out[l], ffn_norm_g[l],
                               w_router[l], w_gate[l], w_up[l], w_down[l])
        y_prompt = _layer(y_prompt, *consts)
        y_sample = _layer(y_sample, *consts)
    return (y_prompt, y_sample)
```

```python
import functools

import jax
import jax.numpy as jnp
from jax import lax
from jax.experimental import pallas as pl
from jax.experimental.pallas import tpu as pltpu

F32 = jnp.float32
BF16 = jnp.bfloat16
I32 = jnp.int32

D_MODEL = 2048
HEAD_DIM = 64
ATTN_WIDTH = 1024
KV_WIDTH = 256
GM_WIDTH = 1024
N_Q_HEADS = 16
N_KV_HEADS = 4
GQA_GROUP = 4
GM_GROUPS = 16
CHUNK = 128
N_EXPERTS = 16
EXPERT_FF = 1024
CAPACITY_FACTOR = 2
EPS = 1e-6
NEG_INF = -1e30

LANES = 128
VMEM_LIMIT = 58 * 1024 * 1024

IN_TOKENS = 512
MIX_TOKENS = 512
FFN_ROWS = 512
FFN_CHUNK = 256
COMB_TOKENS = 512
COMB_PAIRS = 256

_NT = (((1,), (1,)), ((), ()))


def _const_spec(shape):
    nd = len(shape)
    return pl.BlockSpec(shape, lambda *_: (0,) * nd, pipeline_mode=pl.Buffered(1))


def _split3(a):
    a1 = a.astype(BF16)
    r1 = a - a1.astype(F32)
    a2 = r1.astype(BF16)
    a3 = (r1 - a2.astype(F32)).astype(BF16)
    return a1, a2, a3


def _exact_dot(a, b01):
    a1, a2, a3 = _split3(a)
    d = lambda x: jnp.dot(x, b01, preferred_element_type=F32)
    return (d(a1) + d(a2)) + d(a3)


def _exact_dot_left(a01, b):
    b1, b2, b3 = _split3(b)
    d = lambda x: jnp.dot(a01, x, preferred_element_type=F32)
    return (d(b1) + d(b2)) + d(b3)


def _gelu(x):
    return jax.nn.gelu(x)


def _in_proj_kernel(x_ref, ang_ref, wqkv_ref, wugt_ref, qg_ref, kg_ref, gng_ref,
                    q_ref, k_ref, v_ref, ut_ref, gt_ref):
    xf = x_ref[...]
    ms = jnp.mean(xf * xf, axis=-1, keepdims=True)
    h = (xf * lax.rsqrt(ms + EPS) * ang_ref[...]).astype(BF16)
    z = jnp.dot(h, wqkv_ref[...], preferred_element_type=F32)

    low = lax.broadcasted_iota(I32, (1, LANES), 1) < HEAD_DIM

    def two_head_norm(zt, g):
        sq = zt * zt
        s_lo = jnp.sum(jnp.where(low, sq, 0.0), axis=-1, keepdims=True)
        s_hi = jnp.sum(jnp.where(low, 0.0, sq), axis=-1, keepdims=True)
        r = jnp.where(low, lax.rsqrt(s_lo * (1.0 / HEAD_DIM) + EPS), lax.rsqrt(s_hi * (1.0 / HEAD_DIM) + EPS))
        return zt * r * g

    for j in range(ATTN_WIDTH // LANES):
        q_ref[:, j * LANES:(j + 1) * LANES] = two_head_norm(z[:, j * LANES:(j + 1) * LANES], qg_ref[...]).astype(BF16)
    for j in range(KV_WIDTH // LANES):
        c0 = ATTN_WIDTH + j * LANES
        k_ref[:, j * LANES:(j + 1) * LANES] = two_head_norm(z[:, c0:c0 + LANES], kg_ref[...]).astype(BF16)
    v_ref[...] = z[:, ATTN_WIDTH + KV_WIDTH:].astype(BF16)

    zt = lax.dot_general(wugt_ref[...], h, _NT, preferred_element_type=F32)
    ut_ref[...] = _gelu(zt[:GM_WIDTH])
    gl = _gelu(zt[GM_WIDTH:])
    tm = gl.shape[1]
    gl3 = gl.reshape(GM_GROUPS, HEAD_DIM, tm)
    msg = jnp.mean(gl3 * gl3, axis=1, keepdims=True)
    gn = gl3 * lax.rsqrt(msg + EPS) * gng_ref[...]
    gt_ref[...] = gn.reshape(GM_WIDTH, tm).astype(BF16)


def _in_proj(x2d, ang, wqkv, wugt, qg, kg, gng):
    t = x2d.shape[0]
    tm = IN_TOKENS
    assert t % tm == 0
    row = lambda i: (i, 0)
    col = lambda i: (0, i)
    return pl.pallas_call(
        _in_proj_kernel,
        grid=(t // tm,),
        in_specs=[
            pl.BlockSpec((tm, D_MODEL), row),
            _const_spec((1, D_MODEL)),
            _const_spec((D_MODEL, ATTN_WIDTH + 2 * KV_WIDTH)),
            _const_spec((2 * GM_WIDTH, D_MODEL)),
            _const_spec((1, LANES)),
            _const_spec((1, LANES)),
            _const_spec((GM_GROUPS, HEAD_DIM, 1)),
        ],
        out_specs=[
            pl.BlockSpec((tm, ATTN_WIDTH), row),
            pl.BlockSpec((tm, KV_WIDTH), row),
            pl.BlockSpec((tm, KV_WIDTH), row),
            pl.BlockSpec((GM_WIDTH, tm), col),
            pl.BlockSpec((GM_WIDTH, tm), col),
        ],
        out_shape=[
            jax.ShapeDtypeStruct((t, ATTN_WIDTH), BF16),
            jax.ShapeDtypeStruct((t, KV_WIDTH), BF16),
            jax.ShapeDtypeStruct((t, KV_WIDTH), BF16),
            jax.ShapeDtypeStruct((GM_WIDTH, t), F32),
            jax.ShapeDtypeStruct((GM_WIDTH, t), BF16),
        ],
        compiler_params=pltpu.CompilerParams(
            dimension_semantics=("arbitrary",), vmem_limit_bytes=VMEM_LIMIT),
        name="in_proj",
    )(x2d, ang, wqkv, wugt, qg, kg, gng)


def _mixer_kernel(q_ref, kp_ref, kc_ref, kn_ref, vp_ref, vc_ref, vn_ref, ut_ref, gt_ref, x_ref,
                  bias_ref, sink_ref, wsp_ref, bsp_ref, aog_ref, gog_ref, wout_ref, fng_ref,
                  wrs_ref, wrh_ref, x1_ref, aff_ref, *, blocks_per_seq):
    i = pl.program_id(0)
    tb = q_ref.shape[0]
    nsub = tb // CHUNK
    first = (i % blocks_per_seq) == 0
    last = (i % blocks_per_seq) == blocks_per_seq - 1

    kcat = jnp.concatenate([kp_ref[...], kc_ref[...], kn_ref[...]], axis=0)
    vcat = jnp.concatenate([vp_ref[...], vc_ref[...], vn_ref[...]], axis=0)
    key = lax.broadcasted_iota(I32, (1, 3 * CHUNK), 1)

    attn_rows = []
    for sb in range(nsub):
        edge = jnp.zeros((1, 3 * CHUNK), F32)
        if sb == 0:
            edge = jnp.where((key < CHUNK) & first, NEG_INF, edge)
        if sb == nsub - 1:
            edge = jnp.where((key >= 2 * CHUNK) & last, NEG_INF, edge)
        kb = kcat[sb * CHUNK:sb * CHUNK + 3 * CHUNK]
        vb = vcat[sb * CHUNK:sb * CHUNK + 3 * CHUNK]
        qs = q_ref[sb * CHUNK:(sb + 1) * CHUNK, :]
        head_out = []
        for j in range(N_KV_HEADS):
            qg = jnp.concatenate(
                [qs[:, (GQA_GROUP * j + g) * HEAD_DIM:(GQA_GROUP * j + g + 1) * HEAD_DIM] for g in range(GQA_GROUP)],
                axis=0)
            s = lax.dot_general(qg, kb[:, j * HEAD_DIM:(j + 1) * HEAD_DIM], _NT,
                                preferred_element_type=F32) * (HEAD_DIM ** -0.5)
            ps, dens = [], []
            for g in range(GQA_GROUP):
                hd = GQA_GROUP * j + g
                sh = s[g * CHUNK:(g + 1) * CHUNK] + bias_ref[hd] + edge
                snk = sink_ref[hd]
                m = jnp.maximum(jnp.max(sh, axis=-1, keepdims=True), snk)
                p = jnp.exp(sh - m)
                dens.append(jnp.sum(p, axis=-1, keepdims=True) + jnp.exp(snk - m))
                ps.append(p.astype(BF16))
            o = jnp.dot(jnp.concatenate(ps, axis=0), vb[:, j * HEAD_DIM:(j + 1) * HEAD_DIM],
                        preferred_element_type=F32)
            for g in range(GQA_GROUP):
                head_out.append(o[g * CHUNK:(g + 1) * CHUNK] / dens[g])
        attn_rows.append(jnp.concatenate(head_out, axis=1))
    attn = jnp.concatenate(attn_rows, axis=0)
    an = attn * lax.rsqrt(jnp.mean(attn * attn, axis=-1, keepdims=True) + EPS) * aog_ref[...]

    st_rows = []
    for pi in range(GM_GROUPS // 2):
        r0 = pi * 2 * HEAD_DIM
        lhs = jnp.concatenate(
            [jnp.concatenate([gt_ref[r0:r0 + HEAD_DIM, c * CHUNK:(c + 1) * CHUNK],
                              gt_ref[r0 + HEAD_DIM:r0 + 2 * HEAD_DIM, c * CHUNK:(c + 1) * CHUNK]], axis=1)
             for c in range(nsub)], axis=0)
        o = jnp.dot(lhs, wsp_ref[pi], preferred_element_type=F32)
        ga = jnp.concatenate([o[c * HEAD_DIM:(c + 1) * HEAD_DIM, :CHUNK] for c in range(nsub)], axis=1)
        gb = jnp.concatenate([o[c * HEAD_DIM:(c + 1) * HEAD_DIM, CHUNK:] for c in range(nsub)], axis=1)
        ba = jnp.concatenate([bsp_ref[2 * pi]] * nsub, axis=1)
        bb = jnp.concatenate([bsp_ref[2 * pi + 1]] * nsub, axis=1)
        st_rows.append(ga + ba)
        st_rows.append(gb + bb)
    gmt = ut_ref[...] * jnp.concatenate(st_rows, axis=0)
    gm = gmt.T
    gn = gm * lax.rsqrt(jnp.mean(gm * gm, axis=-1, keepdims=True) + EPS) * gog_ref[...]

    cat = jnp.concatenate([an.astype(BF16), gn.astype(BF16)], axis=1)
    x1 = x_ref[...] + jnp.dot(cat, wout_ref[...], preferred_element_type=F32)
    x1_ref[...] = x1

    xn = x1 * lax.rsqrt(jnp.mean(x1 * x1, axis=-1, keepdims=True) + EPS) * fng_ref[...]
    xh = xn.astype(BF16)
    xl = (xn - xh.astype(F32)).astype(BF16)
    r1 = lax.dot_general(wrs_ref[...], xh, _NT, preferred_element_type=F32)
    r2 = lax.dot_general(wrh_ref[...], xl, _NT, preferred_element_type=F32)
    logits = (r1[:N_EXPERTS] + r1[N_EXPERTS:]) + r2
    mx = jnp.max(logits, axis=0, keepdims=True)
    ex = jnp.exp(logits - mx)
    aff_ref[...] = ex / jnp.sum(ex, axis=0, keepdims=True)


def _mixer(q, k, v, ut, gt, x2d, seq_len, consts):
    t = q.shape[0]
    tb = MIX_TOKENS
    assert seq_len % tb == 0 and t % seq_len == 0
    r = tb // CHUNK
    nch = t // CHUNK
    row = lambda i: (i, 0)
    col = lambda i: (0, i)
    prev = lambda i: (jnp.maximum(i * r - 1, 0), 0)
    nxt = lambda i: (jnp.minimum((i + 1) * r, nch - 1), 0)
    (bias, sink, wsp, bsp, aog, gog, wout, fng, wrs, wrh) = consts
    kernel = functools.partial(_mixer_kernel, blocks_per_seq=seq_len // tb)
    return pl.pallas_call(
        kernel,
        grid=(t // tb,),
        in_specs=[
            pl.BlockSpec((tb, ATTN_WIDTH), row),
            pl.BlockSpec((CHUNK, KV_WIDTH), prev),
            pl.BlockSpec((tb, KV_WIDTH), row),
            pl.BlockSpec((CHUNK, KV_WIDTH), nxt),
            pl.BlockSpec((CHUNK, KV_WIDTH), prev),
            pl.BlockSpec((tb, KV_WIDTH), row),
            pl.BlockSpec((CHUNK, KV_WIDTH), nxt),
            pl.BlockSpec((GM_WIDTH, tb), col),
            pl.BlockSpec((GM_WIDTH, tb), col),
            pl.BlockSpec((tb, D_MODEL), row),
            _const_spec((N_Q_HEADS, CHUNK, 3 * CHUNK)),
            pl.BlockSpec(memory_space=pltpu.SMEM),
            _const_spec((GM_GROUPS // 2, 2 * CHUNK, 2 * CHUNK)),
            _const_spec((GM_GROUPS, 1, CHUNK)),
            _const_spec((1, ATTN_WIDTH)),
            _const_spec((1, GM_WIDTH)),
            _const_spec((D_MODEL, D_MODEL)),
            _const_spec((1, D_MODEL)),
            _const_spec((2 * N_EXPERTS, D_MODEL)),
            _const_spec((N_EXPERTS, D_MODEL)),
        ],
        out_specs=[
            pl.BlockSpec((tb, D_MODEL), row),
            pl.BlockSpec((N_EXPERTS, tb), col),
        ],
        out_shape=[
            jax.ShapeDtypeStruct((t, D_MODEL), F32),
            jax.ShapeDtypeStruct((N_EXPERTS, t), F32),
        ],
        compiler_params=pltpu.CompilerParams(
            dimension_semantics=("arbitrary",), vmem_limit_bytes=VMEM_LIMIT),
        name="mixer",
    )(q, k, k, k, v, v, v, ut, gt, x2d, bias, sink, wsp, bsp, aog, gog, wout, fng, wrs, wrh)


def _sum12(x):
    return jnp.sum(jnp.sum(x, axis=2, keepdims=True), axis=1, keepdims=True)


def _select_kernel(a_ref, at_ref, m_ref, mt_ref, ppt_ref, ctok_ref, cnt_ref, *, cap, tok_bits):
    ne, nb, _ = a_ref.shape
    a = a_ref[...]
    capf = jnp.float32(cap)

    def thr_step(it, thr):
        cand = thr | jnp.left_shift(jnp.int32(1), 30 - it)
        cnt = _sum12(jnp.where(a >= lax.bitcast_convert_type(cand, F32), 1.0, 0.0))
        return jnp.where(cnt >= capf, cand, thr)

    thr = lax.bitcast_convert_type(lax.fori_loop(0, 31, thr_step, jnp.zeros((ne, 1, 1), I32)), F32)

    gt = a > thr
    eq = a == thr
    need = capf - _sum12(jnp.where(gt, 1.0, 0.0))
    tok = lax.broadcasted_iota(I32, (1, nb, LANES), 1) * LANES + lax.broadcasted_iota(I32, (1, nb, LANES), 2)

    def cut_step(it, c):
        cand = c + jnp.left_shift(jnp.int32(1), tok_bits - 1 - it)
        f = _sum12(jnp.where(eq & (tok < cand), 1.0, 0.0))
        return jnp.where(f < need, cand, c)

    c = lax.fori_loop(0, tok_bits, cut_step, jnp.zeros((ne, 1, 1), I32))
    tcut = jnp.where(need > 0.0, c + 1, 0)

    m = jnp.where(gt | (eq & (tok < tcut)), 1.0, 0.0)
    m_ref[...] = m.astype(BF16)

    a_t = at_ref[...]
    tok_t = lax.broadcasted_iota(I32, (1, LANES, nb), 2) * LANES + lax.broadcasted_iota(I32, (1, LANES, nb), 1)
    mt = jnp.where((a_t > thr) | ((a_t == thr) & (tok_t < tcut)), 1.0, 0.0)
    mt_ref[...] = mt.astype(BF16)

    cnt_t = jnp.sum(mt, axis=0)
    rl = lax.broadcasted_iota(I32, (LANES, LANES), 0)
    cl = lax.broadcasted_iota(I32, (LANES, LANES), 1)
    tri_strict_low = jnp.where(cl < rl, 1.0, 0.0).astype(BF16)
    local_t = jnp.dot(tri_strict_low, cnt_t.astype(BF16), preferred_element_type=F32)
    tot_t = jnp.sum(cnt_t, axis=0, keepdims=True)
    rb = lax.broadcasted_iota(I32, (nb, nb), 0)
    cb = lax.broadcasted_iota(I32, (nb, nb), 1)
    blk_before = jnp.where(rb < cb, 1.0, 0.0).astype(BF16)
    pb_t = _exact_dot(jnp.broadcast_to(tot_t, (8, nb)), blk_before)[0:1]
    ctok_t = local_t + pb_t
    rank = jnp.zeros((LANES, nb), F32)
    for e in range(ne):
        ppt_ref[e] = ctok_t + rank
        rank = rank + mt[e]

    cnt_b = jnp.sum(m, axis=0)
    tri_strict_up = jnp.where(rl < cl, 1.0, 0.0).astype(BF16)
    local_b = jnp.dot(cnt_b.astype(BF16), tri_strict_up, preferred_element_type=F32)
    totb = jnp.dot(cnt_b.astype(BF16), jnp.ones((LANES, LANES), BF16), preferred_element_type=F32)
    blk_after = jnp.where(cb < rb, 1.0, 0.0).astype(BF16)
    ctok_ref[...] = local_b + _exact_dot_left(blk_after, totb)
    cnt_ref[...] = cnt_b


def _select(a_blk, a_t, cap, tok_bits):
    ne, nb, _ = a_blk.shape
    kernel = functools.partial(_select_kernel, cap=cap, tok_bits=tok_bits)
    full3 = lambda s: pl.BlockSpec(s, lambda: (0, 0, 0))
    full2 = lambda s: pl.BlockSpec(s, lambda: (0, 0))
    return pl.pallas_call(
        kernel,
        in_specs=[full3((ne, nb, LANES)), full3((ne, LANES, nb))],
        out_specs=[full3((ne, nb, LANES)), full3((ne, LANES, nb)), full3((ne, LANES, nb)),
                   full2((nb, LANES)), full2((nb, LANES))],
        out_shape=[
            jax.ShapeDtypeStruct((ne, nb, LANES), BF16),
            jax.ShapeDtypeStruct((ne, LANES, nb), BF16),
            jax.ShapeDtypeStruct((ne, LANES, nb), F32),
            jax.ShapeDtypeStruct((nb, LANES), F32),
            jax.ShapeDtypeStruct((nb, LANES), F32),
        ],
        compiler_params=pltpu.CompilerParams(vmem_limit_bytes=VMEM_LIMIT),
        name="select",
    )(a_blk, a_t)


def _compact_kernel(m_ref, mt_ref, at_ref, ppt_ref, idx_ref, gate_ref, pp_ref):
    nb = m_ref.shape[1]
    nchunks = idx_ref.shape[1]
    mb = m_ref[0]
    mt = mt_ref[0]
    a_t = at_ref[0]
    pp_t = ppt_ref[0]

    totb = jnp.dot(mb, jnp.ones((LANES, LANES), BF16), preferred_element_type=F32)
    rb = lax.broadcasted_iota(I32, (nb, nb), 0)
    cb = lax.broadcasted_iota(I32, (nb, nb), 1)
    blk_after = jnp.where(cb < rb, 1.0, 0.0).astype(BF16)
    cb_excl = jnp.dot(blk_after, totb.astype(BF16), preferred_element_type=F32)
    cb_incl = cb_excl + totb
    rl = lax.broadcasted_iota(I32, (LANES, LANES), 0)
    cl = lax.broadcasted_iota(I32, (LANES, LANES), 1)
    tri_low = jnp.where(cl <= rl, 1.0, 0.0).astype(BF16)
    lt = jnp.dot(tri_low, mt, preferred_element_type=F32).astype(BF16)
    sub_b = lax.broadcasted_iota(I32, (nb, LANES), 0).astype(F32)
    sub_l = rl.astype(F32)
    lane = lax.broadcasted_iota(I32, (1, LANES), 1).astype(F32)

    def chunk(c, carry):
        j = lane + lax.convert_element_type(c * LANES, F32)
        bj = jnp.sum(jnp.where(cb_incl <= j, 1.0, 0.0), axis=0, keepdims=True)
        oh = sub_b == bj
        ohb = jnp.where(oh, 1.0, 0.0).astype(BF16)
        jloc = j - jnp.sum(jnp.where(oh, cb_excl, 0.0), axis=0, keepdims=True)
        glt = jnp.dot(lt, ohb, preferred_element_type=F32)
        lj = jnp.sum(jnp.where(glt <= jloc, 1.0, 0.0), axis=0, keepdims=True)
        ohl = sub_l == lj
        gate = jnp.sum(jnp.where(ohl, _exact_dot(a_t, ohb), 0.0), axis=0, keepdims=True)
        pp = jnp.sum(jnp.where(ohl, _exact_dot(pp_t, ohb), 0.0), axis=0, keepdims=True)
        idx_ref[0, pl.ds(c, 1), :] = (bj * LANES + lj).astype(I32)
        gate_ref[0, pl.ds(c, 1), :] = gate
        pp_ref[0, pl.ds(c, 1), :] = pp.astype(I32)
        return carry

    lax.fori_loop(0, nchunks, chunk, 0)


def _compact(m_blk, m_t, a_t, pp_t, cap):
    ne, nb, _ = m_blk.shape
    assert cap % LANES == 0
    nchunks = cap // LANES
    per_e = lambda s: pl.BlockSpec((1,) + s, lambda e: (e, 0, 0))
    return pl.pallas_call(
        _compact_kernel,
        grid=(ne,),
        in_specs=[per_e((nb, LANES)), per_e((LANES, nb)), per_e((LANES, nb)), per_e((LANES, nb))],
        out_specs=[per_e((nchunks, LANES))] * 3,
        out_shape=[
            jax.ShapeDtypeStruct((ne, nchunks, LANES), I32),
            jax.ShapeDtypeStruct((ne, nchunks, LANES), F32),
            jax.ShapeDtypeStruct((ne, nchunks, LANES), I32),
        ],
        compiler_params=pltpu.CompilerParams(
            dimension_semantics=("arbitrary",), vmem_limit_bytes=VMEM_LIMIT),
        name="compact",
    )(m_blk, m_t, a_t, pp_t)


def _ffn_kernel(idx_ref, idxn_ref, ppp_ref, ppc_ref, gate_ref, x_hbm, fng_ref, wg_ref, wu_ref, wd_ref,
                g_hbm, xbuf, ybuf, gsem, ssem, *, tiles_per_expert, n_pairs):
    tm = xbuf.shape[1]
    step = pl.program_id(0) * tiles_per_expert + pl.program_id(1)
    nsteps = pl.num_programs(0) * tiles_per_expert
    slot = step % 2
    other = 1 - slot

    def gather_copy(row, r, s):
        return pltpu.make_async_copy(x_hbm.at[pl.ds(row, 1), :], xbuf.at[s, pl.ds(r, 1), :], gsem.at[s])

    def scatter_copy(row, r, s):
        return pltpu.make_async_copy(ybuf.at[s, pl.ds(r, 1), :], g_hbm.at[pl.ds(row, 1), :], ssem.at[s])

    def wait_gather(s):
        pltpu.make_async_copy(x_hbm.at[pl.ds(0, tm), :], xbuf.at[s], gsem.at[s]).wait()

    def wait_scatter(s):
        pltpu.make_async_copy(ybuf.at[s], g_hbm.at[pl.ds(0, tm), :], ssem.at[s]).wait()

    @pl.when(step == 0)
    def _():
        def body(r, carry):
            gather_copy(idx_ref[0, 0, r], r, 0).start()
            return carry
        lax.fori_loop(0, tm, body, 0, unroll=8)
        ybuf[1] = jnp.zeros((tm, D_MODEL), F32)

    @pl.when(step > 0)
    def _():
        wait_scatter(slot)

    wait_gather(slot)

    xt = xbuf[slot]
    xn = (xt * lax.rsqrt(jnp.mean(xt * xt, axis=-1, keepdims=True) + EPS) * fng_ref[...]).astype(BF16)

    ri = lax.broadcasted_iota(I32, (tm, tm), 0)
    ci = lax.broadcasted_iota(I32, (tm, tm), 1)
    eye = jnp.where(ri == ci, 1.0, 0.0).astype(BF16)
    g1, g2, g3 = _split3(jnp.broadcast_to(gate_ref[0], (8, tm)))
    nt = lambda g: lax.dot_general(eye, g, _NT, preferred_element_type=F32)
    gcol = ((nt(g1) + nt(g2)) + nt(g3))[:, 0:1]

    first = step == 0
    nchunk = EXPERT_FF // FFN_CHUNK
    per = tm // nchunk
    for c in range(nchunk):
        for r in range(c * per, (c + 1) * per):
            gather_copy(idxn_ref[0, 0, r], r, other).start()
            scatter_copy(jnp.where(first, n_pairs + r, ppp_ref[0, 0, r]), r, other).start()
        cols = slice(c * FFN_CHUNK, (c + 1) * FFN_CHUNK)
        hg = jnp.dot(xn, wg_ref[0, :, cols], preferred_element_type=F32)
        hu = jnp.dot(xn, wu_ref[0, :, cols], preferred_element_type=F32)
        hh = (jax.nn.silu(hg) * hu).astype(BF16)
        part = jnp.dot(hh, wd_ref[0, cols, :], preferred_element_type=F32)
        if c == 0:
            ybuf[slot] = part
        elif c < nchunk - 1:
            ybuf[slot] = ybuf[slot] + part
        else:
            ybuf[slot] = (ybuf[slot] + part) * gcol

    @pl.when(step + 1 == nsteps)
    def _():
        def body(r, carry):
            scatter_copy(ppc_ref[0, 0, r], r, slot).start()
            return carry
        lax.fori_loop(0, tm, body, 0, unroll=8)
        wait_scatter(other)
        wait_scatter(slot)
        wait_gather(other)


def _ffn(idx_t, gate_t, pp_t, x1, fng, wg, wu, wd, n_pairs):
    ntiles, _, tm = idx_t.shape
    tpe = ntiles // N_EXPERTS
    assert tm % (EXPERT_FF // FFN_CHUNK) == 0
    cur = lambda e, i: (e * tpe + i, 0, 0)
    nxt = lambda e, i: (jnp.minimum(e * tpe + i + 1, ntiles - 1), 0, 0)
    prv = lambda e, i: (jnp.maximum(e * tpe + i - 1, 0), 0, 0)
    wmap = lambda e, i: (e, 0, 0)
    kernel = functools.partial(_ffn_kernel, tiles_per_expert=tpe, n_pairs=n_pairs)
    return pl.pallas_call(
        kernel,
        grid=(N_EXPERTS, tpe),
        in_specs=[
            pl.BlockSpec((1, 1, tm), cur, memory_space=pltpu.SMEM),
            pl.BlockSpec((1, 1, tm), nxt, memory_space=pltpu.SMEM),
            pl.BlockSpec((1, 1, tm), prv, memory_space=pltpu.SMEM),
            pl.BlockSpec((1, 1, tm), cur, memory_space=pltpu.SMEM),
            pl.BlockSpec((1, 1, tm), cur),
            pl.BlockSpec(memory_space=pl.ANY),
            _const_spec((1, D_MODEL)),
            pl.BlockSpec((1, D_MODEL, EXPERT_FF), wmap),
            pl.BlockSpec((1, D_MODEL, EXPERT_FF), wmap),
            pl.BlockSpec((1, EXPERT_FF, D_MODEL), wmap),
        ],
        out_specs=pl.BlockSpec(memory_space=pl.ANY),
        out_shape=jax.ShapeDtypeStruct((n_pairs + tm, D_MODEL), F32),
        scratch_shapes=[
            pltpu.VMEM((2, tm, D_MODEL), F32),
            pltpu.VMEM((2, tm, D_MODEL), F32),
            pltpu.SemaphoreType.DMA((2,)),
            pltpu.SemaphoreType.DMA((2,)),
        ],
        compiler_params=pltpu.CompilerParams(
            dimension_semantics=("arbitrary", "arbitrary"), vmem_limit_bytes=VMEM_LIMIT),
        name="ffn",
    )(idx_t, idx_t, pp_t, pp_t, gate_t, x1, fng, wg, wu, wd)


def _combine_kernel(boff_ref, x1_ref, ctok_ref, cnt_ref, g_hbm, y_ref, gbuf, acc_ref, sem, st_ref, *, n_pairs):
    i = pl.program_id(0)
    nsteps = pl.num_programs(0)
    tb = x1_ref.shape[0]
    pc = gbuf.shape[1]
    r = tb // LANES

    def chunk_range(blk):
        p_lo = boff_ref[blk]
        p_hi = boff_ref[blk + 1]
        start = (p_lo // 8) * 8
        return start, (p_hi - start + pc - 1) // pc

    def chunk_copy(start, c, s):
        base = jnp.minimum(start + c * pc, n_pairs - pc)
        return pltpu.make_async_copy(g_hbm.at[pl.ds(base, pc), :], gbuf.at[s], sem.at[s])

    start, nch = chunk_range(i)
    start_n, nch_n = chunk_range(jnp.minimum(i + 1, nsteps - 1))
    has_next = (i + 1 < nsteps) & (nch_n > 0)

    @pl.when(i == 0)
    def _():
        st_ref[0] = 0

        @pl.when(nch > 0)
        def _():
            chunk_copy(start, 0, 0).start()

    s0 = st_ref[0]

    ri = lax.broadcasted_iota(I32, (LANES, LANES), 0)
    ci = lax.broadcasted_iota(I32, (LANES, LANES), 1)
    eye = jnp.where(ri == ci, 1.0, 0.0).astype(BF16)

    def to_col(rowv):
        p1, p2, p3 = _split3(jnp.broadcast_to(rowv, (8, LANES)))
        nt = lambda p: lax.dot_general(eye, p, _NT, preferred_element_type=F32)
        return ((nt(p1) + nt(p2)) + nt(p3))[:, 0:1]

    cex_rows = ctok_ref[pl.ds(i * r, r), :]
    cnt_rows = cnt_ref[pl.ds(i * r, r), :]
    cex = jnp.concatenate([to_col(cex_rows[k:k + 1]) for k in range(r)], axis=0)
    cin = cex + jnp.concatenate([to_col(cnt_rows[k:k + 1]) for k in range(r)], axis=0)

    acc_ref[...] = jnp.zeros_like(acc_ref)

    def body(c, carry):
        s = (s0 + c) % 2
        chunk_copy(start, c, s).wait()

        @pl.when(c + 1 < nch)
        def _():
            chunk_copy(start, c + 1, 1 - s).start()

        @pl.when((c + 1 == nch) & has_next)
        def _():
            chunk_copy(start_n, 0, 1 - s).start()

        want = start + c * pc
        base = jnp.minimum(want, n_pairs - pc)
        p = (base + lax.broadcasted_iota(I32, (1, pc), 1)).astype(F32)
        sel = (cex <= p) & (p < cin) & (p >= lax.convert_element_type(want, F32))
        acc_ref[...] += jnp.dot(jnp.where(sel, 1.0, 0.0).astype(BF16), gbuf[s].astype(BF16),
                                preferred_element_type=F32)
        return carry

    lax.fori_loop(0, nch, body, 0)

    @pl.when((nch == 0) & has_next)
    def _():
        chunk_copy(start_n, 0, s0).start()

    st_ref[0] = (s0 + nch) % 2
    y_ref[...] = x1_ref[...] + acc_ref[...]


def _combine(boff, x1, ctok, cnt, g, n_pairs):
    t = x1.shape[0]
    tb = COMB_TOKENS
    nb = ctok.shape[0]
    assert t % tb == 0 and n_pairs >= COMB_PAIRS
    kernel = functools.partial(_combine_kernel, n_pairs=n_pairs)
    grid_spec = pltpu.PrefetchScalarGridSpec(
        num_scalar_prefetch=1,
        grid=(t // tb,),
        in_specs=[
            pl.BlockSpec((tb, D_MODEL), lambda i, b: (i, 0)),
            pl.BlockSpec((nb, LANES), lambda i, b: (0, 0)),
            pl.BlockSpec((nb, LANES), lambda i, b: (0, 0)),
            pl.BlockSpec(memory_space=pl.ANY),
        ],
        out_specs=pl.BlockSpec((tb, D_MODEL), lambda i, b: (i, 0)),
        scratch_shapes=[
            pltpu.VMEM((2, COMB_PAIRS, D_MODEL), F32),
            pltpu.VMEM((tb, D_MODEL), F32),
            pltpu.SemaphoreType.DMA((2,)),
            pltpu.SMEM((1,), I32),
        ],
    )
    return pl.pallas_call(
        kernel,
        grid_spec=grid_spec,
        out_shape=jax.ShapeDtypeStruct((t, D_MODEL), F32),
        compiler_params=pltpu.CompilerParams(
            dimension_semantics=("arbitrary",), vmem_limit_bytes=VMEM_LIMIT),
        name="combine",
    )(boff, x1, ctok, cnt, g)


def _alibi_bias():
    slopes = 2.0 ** (-8.0 * jnp.arange(1, N_Q_HEADS + 1, dtype=F32) / N_Q_HEADS)
    qi = jnp.arange(CHUNK)[:, None]
    kj = jnp.arange(3 * CHUNK)[None, :]
    rel = jnp.abs(qi + CHUNK - kj)
    bias = -slopes[:, None, None] * rel.astype(F32)[None]
    return jnp.where((rel <= CHUNK)[None], bias, NEG_INF)


def _prep_weights(attn_norm_g, w_in, q_norm_g, k_norm_g, sink, gm_norm_g, w_spatial, b_spatial,
                  attn_out_g, gm_out_g, w_out, ffn_norm_g, w_router, w_gate, w_up, w_down):
    nqkv = ATTN_WIDTH + 2 * KV_WIDTH
    ang = attn_norm_g.reshape(1, D_MODEL)
    wqkv = w_in[:, :nqkv].astype(BF16)
    wugt = w_in[:, nqkv:].T.astype(BF16)
    qg = jnp.tile(q_norm_g, 2).reshape(1, LANES)
    kg = jnp.tile(k_norm_g, 2).reshape(1, LANES)
    gng = gm_norm_g.reshape(GM_GROUPS, HEAD_DIM, 1)
    wst = jnp.swapaxes(w_spatial, 1, 2).astype(BF16)
    zero = jnp.zeros((GM_GROUPS // 2, CHUNK, CHUNK), BF16)
    wsp = jnp.concatenate([jnp.concatenate([wst[0::2], zero], axis=2),
                           jnp.concatenate([zero, wst[1::2]], axis=2)], axis=1)
    bsp = b_spatial.reshape(GM_GROUPS, 1, CHUNK)
    aog = attn_out_g.reshape(1, ATTN_WIDTH)
    gog = gm_out_g.reshape(1, GM_WIDTH)
    wout = w_out.astype(BF16)
    fng = ffn_norm_g.reshape(1, D_MODEL)
    wr_t = w_router.T
    wr_hi = wr_t.astype(BF16)
    wr_lo = (wr_t - wr_hi.astype(F32)).astype(BF16)
    wrs = jnp.concatenate([wr_hi, wr_lo], axis=0)
    mixer_consts = (_alibi_bias(), sink, wsp, bsp, aog, gog, wout, fng, wrs, wr_hi)
    in_consts = (ang, wqkv, wugt, qg, kg, gng)
    ffn_consts = (fng, w_gate.astype(BF16), w_up.astype(BF16), w_down.astype(BF16))
    return in_consts, mixer_consts, ffn_consts


def _layer(x, in_consts, mixer_consts, ffn_consts):
    b, s, d = x.shape
    t = b * s
    x2d = x.reshape(t, d)
    q, k, v, ut, gt = _in_proj(x2d, *in_consts)
    x1, aff_t = _mixer(q, k, v, ut, gt, x2d, s, mixer_consts)

    nb = t // LANES
    cap = max(1, CAPACITY_FACTOR * t // N_EXPERTS)
    tok_bits = max(1, (t - 1).bit_length())
    a_blk = aff_t.reshape(N_EXPERTS, nb, LANES)
    a_t = jnp.swapaxes(a_blk, 1, 2)
    m_blk, m_t, pp_t, ctok, cnt = _select(a_blk, a_t, cap, tok_bits)
    idx, gate, pp = _compact(m_blk, m_t, a_t, pp_t, cap)

    tm = min(FFN_ROWS, cap)
    ntiles = N_EXPERTS * cap // tm
    n_pairs = N_EXPERTS * cap
    g = _ffn(idx.reshape(ntiles, 1, tm), gate.reshape(ntiles, 1, tm), pp.reshape(ntiles, 1, tm),
             x1, *ffn_consts, n_pairs)

    boff = jnp.concatenate([ctok.reshape(-1)[::COMB_TOKENS].astype(I32), jnp.full((1,), n_pairs, I32)])
    y = _combine(boff, x1, ctok, cnt, g, n_pairs)
    return y.reshape(b, s, d)


def kernel(x_prompt, x_sample, attn_norm_g, w_in, q_norm_g, k_norm_g, sink, gm_norm_g, w_spatial, b_spatial, attn_out_g, gm_out_g, w_out, ffn_norm_g, w_router, w_gate, w_up, w_down):
    y_prompt, y_sample = x_prompt, x_sample
    for l in range(w_in.shape[0]):
        consts = _prep_weights(attn_norm_g[l], w_in[l], q_norm_g[l], k_norm_g[l], sink[l], gm_norm_g[l],
                               w_spatial[l], b_spatial[l], attn_out_g[l], gm_out_g[l], w_out[l], ffn_norm_g[l],
                               w_router[l], w_gate[l], w_up[l], w_down[l])
        y_prompt = _layer(y_prompt, *consts)
        y_sample = _layer(y_sample, *consts)
    return (y_prompt, y_sample)
```

```python
import functools

import jax
import jax.numpy as jnp
from jax import lax
from jax.experimental import pallas as pl
from jax.experimental.pallas import tpu as pltpu

F32 = jnp.float32
BF16 = jnp.bfloat16
I32 = jnp.int32

D_MODEL = 2048
HEAD_DIM = 64
ATTN_WIDTH = 1024
KV_WIDTH = 256
GM_WIDTH = 1024
N_Q_HEADS = 16
N_KV_HEADS = 4
GQA_GROUP = 4
GM_GROUPS = 16
CHUNK = 128
N_EXPERTS = 16
EXPERT_FF = 1024
CAPACITY_FACTOR = 2
EPS = 1e-6
NEG_INF = -1e30
LOG2E = 1.4426950408889634
Q_SCALE = HEAD_DIM ** -0.5 * LOG2E

LANES = 128
VMEM_LIMIT = 58 * 1024 * 1024

IN_TOKENS = 512
MIX_TOKENS = 512
FFN_ROWS = 512
FFN_CHUNK = 256
COMB_TOKENS = 512
COMB_PAIRS = 256
COMB_SLOTS = 4

_NT = (((1,), (1,)), ((), ()))


def _const_spec(shape):
    nd = len(shape)
    return pl.BlockSpec(shape, lambda *_: (0,) * nd, pipeline_mode=pl.Buffered(1))


def _split3(a):
    a1 = a.astype(BF16)
    r1 = a - a1.astype(F32)
    a2 = r1.astype(BF16)
    a3 = (r1 - a2.astype(F32)).astype(BF16)
    return a1, a2, a3


def _exact_dot(a, b01):
    a1, a2, a3 = _split3(a)
    d = lambda x: jnp.dot(x, b01, preferred_element_type=F32)
    return (d(a1) + d(a2)) + d(a3)


def _exact_dot_left(a01, b):
    b1, b2, b3 = _split3(b)
    d = lambda x: jnp.dot(a01, x, preferred_element_type=F32)
    return (d(b1) + d(b2)) + d(b3)


def _gelu(x):
    return jax.nn.gelu(x)


def _in_proj_kernel(x_ref, ang_ref, wqkv_ref, wugt_ref, qg_ref, kg_ref, gng_ref,
                    q_ref, k_ref, v_ref, ut_ref, gt_ref):
    xf = x_ref[...]
    ms = jnp.mean(xf * xf, axis=-1, keepdims=True)
    h = (xf * lax.rsqrt(ms + EPS) * ang_ref[...]).astype(BF16)
    z = jnp.dot(h, wqkv_ref[...], preferred_element_type=F32)

    low = lax.broadcasted_iota(I32, (1, LANES), 1) < HEAD_DIM

    def two_head_norm(zt, g):
        sq = zt * zt
        s_lo = jnp.sum(jnp.where(low, sq, 0.0), axis=-1, keepdims=True)
        s_hi = jnp.sum(jnp.where(low, 0.0, sq), axis=-1, keepdims=True)
        r = jnp.where(low, lax.rsqrt(s_lo * (1.0 / HEAD_DIM) + EPS), lax.rsqrt(s_hi * (1.0 / HEAD_DIM) + EPS))
        return zt * r * g

    for j in range(ATTN_WIDTH // LANES):
        qn = two_head_norm(z[:, j * LANES:(j + 1) * LANES], qg_ref[...])
        q_ref[:, j * LANES:(j + 1) * LANES] = (qn * Q_SCALE).astype(BF16)
    for j in range(KV_WIDTH // LANES):
        c0 = ATTN_WIDTH + j * LANES
        k_ref[:, j * LANES:(j + 1) * LANES] = two_head_norm(z[:, c0:c0 + LANES], kg_ref[...]).astype(BF16)
    v_ref[...] = z[:, ATTN_WIDTH + KV_WIDTH:].astype(BF16)

    zt = lax.dot_general(wugt_ref[...], h, _NT, preferred_element_type=F32)
    ut_ref[...] = _gelu(zt[:GM_WIDTH])
    gl = _gelu(zt[GM_WIDTH:])
    tm = gl.shape[1]
    gl3 = gl.reshape(GM_GROUPS, HEAD_DIM, tm)
    msg = jnp.mean(gl3 * gl3, axis=1, keepdims=True)
    gn = gl3 * lax.rsqrt(msg + EPS) * gng_ref[...]
    gt_ref[...] = gn.reshape(GM_WIDTH, tm).astype(BF16)


def _in_proj(x2d, ang, wqkv, wugt, qg, kg, gng):
    t = x2d.shape[0]
    tm = IN_TOKENS
    assert t % tm == 0
    row = lambda i: (i, 0)
    col = lambda i: (0, i)
    return pl.pallas_call(
        _in_proj_kernel,
        grid=(t // tm,),
        in_specs=[
            pl.BlockSpec((tm, D_MODEL), row),
            _const_spec((1, D_MODEL)),
            _const_spec((D_MODEL, ATTN_WIDTH + 2 * KV_WIDTH)),
            _const_spec((2 * GM_WIDTH, D_MODEL)),
            _const_spec((1, LANES)),
            _const_spec((1, LANES)),
            _const_spec((GM_GROUPS, HEAD_DIM, 1)),
        ],
        out_specs=[
            pl.BlockSpec((tm, ATTN_WIDTH), row),
            pl.BlockSpec((tm, KV_WIDTH), row),
            pl.BlockSpec((tm, KV_WIDTH), row),
            pl.BlockSpec((GM_WIDTH, tm), col),
            pl.BlockSpec((GM_WIDTH, tm), col),
        ],
        out_shape=[
            jax.ShapeDtypeStruct((t, ATTN_WIDTH), BF16),
            jax.ShapeDtypeStruct((t, KV_WIDTH), BF16),
            jax.ShapeDtypeStruct((t, KV_WIDTH), BF16),
            jax.ShapeDtypeStruct((GM_WIDTH, t), F32),
            jax.ShapeDtypeStruct((GM_WIDTH, t), BF16),
        ],
        compiler_params=pltpu.CompilerParams(
            dimension_semantics=("arbitrary",), vmem_limit_bytes=VMEM_LIMIT),
        name="in_proj",
    )(x2d, ang, wqkv, wugt, qg, kg, gng)


def _mixer_kernel(q_ref, kp_ref, kc_ref, kn_ref, vp_ref, vc_ref, vn_ref, ut_ref, gt_ref, x_ref,
                  bias_ref, sink_ref, wsp_ref, bsp_ref, aog_ref, gog_ref, wout_ref, fng_ref,
                  wrs_ref, wrh_ref, x1_ref, aff_ref, *, blocks_per_seq):
    i = pl.program_id(0)
    tb = q_ref.shape[0]
    nsub = tb // CHUNK
    first = (i % blocks_per_seq) == 0
    last = (i % blocks_per_seq) == blocks_per_seq - 1

    kcat = jnp.concatenate([kp_ref[...], kc_ref[...], kn_ref[...]], axis=0)
    vcat = jnp.concatenate([vp_ref[...], vc_ref[...], vn_ref[...]], axis=0)
    key = lax.broadcasted_iota(I32, (1, 3 * CHUNK), 1)
    low = lax.broadcasted_iota(I32, (1, LANES), 1) < HEAD_DIM
    zero = jnp.zeros((), BF16)
    ones_bd = jnp.where((lax.broadcasted_iota(I32, (6 * CHUNK, LANES), 0) < 3 * CHUNK)
                        == (lax.broadcasted_iota(I32, (6 * CHUNK, LANES), 1) < HEAD_DIM), 1.0, 0.0).astype(BF16)

    def swap_halves(t):
        return jnp.concatenate([t[:, HEAD_DIM:], t[:, :HEAD_DIM]], axis=1)

    attn_rows = []
    for sb in range(nsub):
        edge = None
        if sb == 0:
            edge = jnp.where((key < CHUNK) & first, NEG_INF, 0.0)
        if sb == nsub - 1:
            hi = jnp.where((key >= 2 * CHUNK) & last, NEG_INF, 0.0)
            edge = hi if edge is None else edge + hi
        tiles = []
        for jt in range(N_KV_HEADS // 2):
            kt = kcat[sb * CHUNK:sb * CHUNK + 3 * CHUNK, jt * LANES:(jt + 1) * LANES]
            vt = vcat[sb * CHUNK:sb * CHUNK + 3 * CHUNK, jt * LANES:(jt + 1) * LANES]
            kts, vts = swap_halves(kt), swap_halves(vt)
            for jj in range(2):
                j = 2 * jt + jj
                ksame, kswap = (kt, kts) if jj == 0 else (kts, kt)
                vsame, vswap = (vt, vts) if jj == 0 else (vts, vt)
                krhs = jnp.concatenate([jnp.where(low, ksame, zero), jnp.where(low, zero, kswap)], axis=0)
                vrhs = jnp.concatenate([jnp.where(low, vsame, zero), jnp.where(low, zero, vswap)], axis=0)
                vaug = jnp.concatenate([vrhs, ones_bd], axis=1)
                for pr in range(GQA_GROUP // 2):
                    ha = GQA_GROUP * j + 2 * pr
                    qp = q_ref[sb * CHUNK:(sb + 1) * CHUNK, ha * HEAD_DIM:(ha + 2) * HEAD_DIM]
                    s = lax.dot_general(qp, krhs, _NT, preferred_element_type=F32)
                    ps, es = [], []
                    for g in range(2):
                        sh = s[:, g * 3 * CHUNK:(g + 1) * 3 * CHUNK] + bias_ref[ha + g]
                        if edge is not None:
                            sh = sh + edge
                        snk = sink_ref[ha + g]
                        m = jnp.maximum(jnp.max(sh, axis=-1, keepdims=True), snk)
                        ps.append(jnp.exp2(sh - m).astype(BF16))
                        es.append(jnp.exp2(snk - m))
                    o = jnp.dot(jnp.concatenate(ps, axis=1), vaug, preferred_element_type=F32)
                    den = o[:, LANES:] + jnp.where(low, es[0], es[1])
                    tiles.append(o[:, :LANES] / den)
        attn_rows.append(jnp.concatenate(tiles, axis=1))
    attn = jnp.concatenate(attn_rows, axis=0)
    an = attn * lax.rsqrt(jnp.mean(attn * attn, axis=-1, keepdims=True) + EPS) * aog_ref[...]

    st_rows = []
    for pi in range(GM_GROUPS // 2):
        r0 = pi * 2 * HEAD_DIM
        lhs = jnp.concatenate(
            [jnp.concatenate([gt_ref[r0:r0 + HEAD_DIM, c * CHUNK:(c + 1) * CHUNK],
                              gt_ref[r0 + HEAD_DIM:r0 + 2 * HEAD_DIM, c * CHUNK:(c + 1) * CHUNK]], axis=1)
             for c in range(nsub)], axis=0)
        o = jnp.dot(lhs, wsp_ref[pi], preferred_element_type=F32)
        ga = jnp.concatenate([o[c * HEAD_DIM:(c + 1) * HEAD_DIM, :CHUNK] for c in range(nsub)], axis=1)
        gb = jnp.concatenate([o[c * HEAD_DIM:(c + 1) * HEAD_DIM, CHUNK:] for c in range(nsub)], axis=1)
        ba = jnp.concatenate([bsp_ref[2 * pi]] * nsub, axis=1)
        bb = jnp.concatenate([bsp_ref[2 * pi + 1]] * nsub, axis=1)
        st_rows.append(ga + ba)
        st_rows.append(gb + bb)
    gmt = ut_ref[...] * jnp.concatenate(st_rows, axis=0)
    gm = gmt.T
    gn = gm * lax.rsqrt(jnp.mean(gm * gm, axis=-1, keepdims=True) + EPS) * gog_ref[...]

    cat = jnp.concatenate([an.astype(BF16), gn.astype(BF16)], axis=1)
    x1 = x_ref[...] + jnp.dot(cat, wout_ref[...], preferred_element_type=F32)
    x1_ref[...] = x1

    xn = x1 * lax.rsqrt(jnp.mean(x1 * x1, axis=-1, keepdims=True) + EPS) * fng_ref[...]
    xh = xn.astype(BF16)
    xl = (xn - xh.astype(F32)).astype(BF16)
    r1 = lax.dot_general(wrs_ref[...], xh, _NT, preferred_element_type=F32)
    r2 = lax.dot_general(wrh_ref[...], xl, _NT, preferred_element_type=F32)
    logits = (r1[:N_EXPERTS] + r1[N_EXPERTS:]) + r2
    mx = jnp.max(logits, axis=0, keepdims=True)
    ex = jnp.exp(logits - mx)
    aff_ref[...] = ex / jnp.sum(ex, axis=0, keepdims=True)


def _mixer(q, k, v, ut, gt, x2d, seq_len, consts):
    t = q.shape[0]
    tb = MIX_TOKENS
    assert seq_len % tb == 0 and t % seq_len == 0
    r = tb // CHUNK
    nch = t // CHUNK
    row = lambda i: (i, 0)
    col = lambda i: (0, i)
    prev = lambda i: (jnp.maximum(i * r - 1, 0), 0)
    nxt = lambda i: (jnp.minimum((i + 1) * r, nch - 1), 0)
    (bias, sink, wsp, bsp, aog, gog, wout, fng, wrs, wrh) = consts
    kernel = functools.partial(_mixer_kernel, blocks_per_seq=seq_len // tb)
    return pl.pallas_call(
        kernel,
        grid=(t // tb,),
        in_specs=[
            pl.BlockSpec((tb, ATTN_WIDTH), row),
            pl.BlockSpec((CHUNK, KV_WIDTH), prev),
            pl.BlockSpec((tb, KV_WIDTH), row),
            pl.BlockSpec((CHUNK, KV_WIDTH), nxt),
            pl.BlockSpec((CHUNK, KV_WIDTH), prev),
            pl.BlockSpec((tb, KV_WIDTH), row),
            pl.BlockSpec((CHUNK, KV_WIDTH), nxt),
            pl.BlockSpec((GM_WIDTH, tb), col),
            pl.BlockSpec((GM_WIDTH, tb), col),
            pl.BlockSpec((tb, D_MODEL), row),
            _const_spec((N_Q_HEADS, CHUNK, 3 * CHUNK)),
            pl.BlockSpec(memory_space=pltpu.SMEM),
            _const_spec((GM_GROUPS // 2, 2 * CHUNK, 2 * CHUNK)),
            _const_spec((GM_GROUPS, 1, CHUNK)),
            _const_spec((1, ATTN_WIDTH)),
            _const_spec((1, GM_WIDTH)),
            _const_spec((D_MODEL, D_MODEL)),
            _const_spec((1, D_MODEL)),
            _const_spec((2 * N_EXPERTS, D_MODEL)),
            _const_spec((N_EXPERTS, D_MODEL)),
        ],
        out_specs=[
            pl.BlockSpec((tb, D_MODEL), row),
            pl.BlockSpec((N_EXPERTS, tb), col),
        ],
        out_shape=[
            jax.ShapeDtypeStruct((t, D_MODEL), F32),
            jax.ShapeDtypeStruct((N_EXPERTS, t), F32),
        ],
        compiler_params=pltpu.CompilerParams(
            dimension_semantics=("arbitrary",), vmem_limit_bytes=VMEM_LIMIT),
        name="mixer",
    )(q, k, k, k, v, v, v, ut, gt, x2d, bias, sink, wsp, bsp, aog, gog, wout, fng, wrs, wrh)


def _sum12(x):
    return jnp.sum(jnp.sum(x, axis=2, keepdims=True), axis=1, keepdims=True)


def _select_kernel(a_ref, at_ref, m_ref, mt_ref, ppt_ref, ctok_ref, cnt_ref, *, cap, tok_bits):
    ne, nb, _ = a_ref.shape
    a = a_ref[...]
    capf = jnp.float32(cap)

    def thr_step(it, thr):
        cand = thr | jnp.left_shift(jnp.int32(1), 30 - it)
        cnt = _sum12(jnp.where(a >= lax.bitcast_convert_type(cand, F32), 1.0, 0.0))
        return jnp.where(cnt >= capf, cand, thr)

    thr = lax.bitcast_convert_type(lax.fori_loop(0, 31, thr_step, jnp.zeros((ne, 1, 1), I32)), F32)

    gt = a > thr
    eq = a == thr
    need = capf - _sum12(jnp.where(gt, 1.0, 0.0))
    tok = lax.broadcasted_iota(I32, (1, nb, LANES), 1) * LANES + lax.broadcasted_iota(I32, (1, nb, LANES), 2)

    def cut_step(it, c):
        cand = c + jnp.left_shift(jnp.int32(1), tok_bits - 1 - it)
        f = _sum12(jnp.where(eq & (tok < cand), 1.0, 0.0))
        return jnp.where(f < need, cand, c)

    c = lax.fori_loop(0, tok_bits, cut_step, jnp.zeros((ne, 1, 1), I32))
    tcut = jnp.where(need > 0.0, c + 1, 0)

    m = jnp.where(gt | (eq & (tok < tcut)), 1.0, 0.0)
    m_ref[...] = m.astype(BF16)

    a_t = at_ref[...]
    tok_t = lax.broadcasted_iota(I32, (1, LANES, nb), 2) * LANES + lax.broadcasted_iota(I32, (1, LANES, nb), 1)
    mt = jnp.where((a_t > thr) | ((a_t == thr) & (tok_t < tcut)), 1.0, 0.0)
    mt_ref[...] = mt.astype(BF16)

    cnt_t = jnp.sum(mt, axis=0)
    rl = lax.broadcasted_iota(I32, (LANES, LANES), 0)
    cl = lax.broadcasted_iota(I32, (LANES, LANES), 1)
    tri_strict_low = jnp.where(cl < rl, 1.0, 0.0).astype(BF16)
    local_t = jnp.dot(tri_strict_low, cnt_t.astype(BF16), preferred_element_type=F32)
    tot_t = jnp.sum(cnt_t, axis=0, keepdims=True)
    rb = lax.broadcasted_iota(I32, (nb, nb), 0)
    cb = lax.broadcasted_iota(I32, (nb, nb), 1)
    blk_before = jnp.where(rb < cb, 1.0, 0.0).astype(BF16)
    pb_t = _exact_dot(jnp.broadcast_to(tot_t, (8, nb)), blk_before)[0:1]
    ctok_t = local_t + pb_t
    rank = jnp.zeros((LANES, nb), F32)
    for e in range(ne):
        ppt_ref[e] = ctok_t + rank
        rank = rank + mt[e]

    cnt_b = jnp.sum(m, axis=0)
    tri_strict_up = jnp.where(rl < cl, 1.0, 0.0).astype(BF16)
    local_b = jnp.dot(cnt_b.astype(BF16), tri_strict_up, preferred_element_type=F32)
    totb = jnp.dot(cnt_b.astype(BF16), jnp.ones((LANES, LANES), BF16), preferred_element_type=F32)
    blk_after = jnp.where(cb < rb, 1.0, 0.0).astype(BF16)
    ctok_ref[...] = local_b + _exact_dot_left(blk_after, totb)
    cnt_ref[...] = cnt_b


def _select(a_blk, a_t, cap, tok_bits):
    ne, nb, _ = a_blk.shape
    kernel = functools.partial(_select_kernel, cap=cap, tok_bits=tok_bits)
    full3 = lambda s: pl.BlockSpec(s, lambda: (0, 0, 0))
    full2 = lambda s: pl.BlockSpec(s, lambda: (0, 0))
    return pl.pallas_call(
        kernel,
        in_specs=[full3((ne, nb, LANES)), full3((ne, LANES, nb))],
        out_specs=[full3((ne, nb, LANES)), full3((ne, LANES, nb)), full3((ne, LANES, nb)),
                   full2((nb, LANES)), full2((nb, LANES))],
        out_shape=[
            jax.ShapeDtypeStruct((ne, nb, LANES), BF16),
            jax.ShapeDtypeStruct((ne, LANES, nb), BF16),
            jax.ShapeDtypeStruct((ne, LANES, nb), F32),
            jax.ShapeDtypeStruct((nb, LANES), F32),
            jax.ShapeDtypeStruct((nb, LANES), F32),
        ],
        compiler_params=pltpu.CompilerParams(vmem_limit_bytes=VMEM_LIMIT),
        name="select",
    )(a_blk, a_t)


def _compact_kernel(m_ref, mt_ref, at_ref, ppt_ref, idx_ref, gate_ref, pp_ref):
    nb = m_ref.shape[1]
    nchunks = idx_ref.shape[1]
    mb = m_ref[0]
    mt = mt_ref[0]
    a_t = at_ref[0]
    pp_t = ppt_ref[0]

    totb = jnp.dot(mb, jnp.ones((LANES, LANES), BF16), preferred_element_type=F32)
    rb = lax.broadcasted_iota(I32, (nb, nb), 0)
    cb = lax.broadcasted_iota(I32, (nb, nb), 1)
    blk_after = jnp.where(cb < rb, 1.0, 0.0).astype(BF16)
    cb_excl = jnp.dot(blk_after, totb.astype(BF16), preferred_element_type=F32)
    cb_incl = cb_excl + totb
    rl = lax.broadcasted_iota(I32, (LANES, LANES), 0)
    cl = lax.broadcasted_iota(I32, (LANES, LANES), 1)
    tri_low = jnp.where(cl <= rl, 1.0, 0.0).astype(BF16)
    lt = jnp.dot(tri_low, mt, preferred_element_type=F32).astype(BF16)
    sub_b = lax.broadcasted_iota(I32, (nb, LANES), 0).astype(F32)
    sub_l = rl.astype(F32)
    lane = lax.broadcasted_iota(I32, (1, LANES), 1).astype(F32)

    def chunk(c, carry):
        j = lane + lax.convert_element_type(c * LANES, F32)
        bj = jnp.sum(jnp.where(cb_incl <= j, 1.0, 0.0), axis=0, keepdims=True)
        oh = sub_b == bj
        ohb = jnp.where(oh, 1.0, 0.0).astype(BF16)
        jloc = j - jnp.sum(jnp.where(oh, cb_excl, 0.0), axis=0, keepdims=True)
        glt = jnp.dot(lt, ohb, preferred_element_type=F32)
        lj = jnp.sum(jnp.where(glt <= jloc, 1.0, 0.0), axis=0, keepdims=True)
        ohl = sub_l == lj
        gate = jnp.sum(jnp.where(ohl, _exact_dot(a_t, ohb), 0.0), axis=0, keepdims=True)
        pp = jnp.sum(jnp.where(ohl, _exact_dot(pp_t, ohb), 0.0), axis=0, keepdims=True)
        idx_ref[0, pl.ds(c, 1), :] = (bj * LANES + lj).astype(I32)
        gate_ref[0, pl.ds(c, 1), :] = gate
        pp_ref[0, pl.ds(c, 1), :] = pp.astype(I32)
        return carry

    lax.fori_loop(0, nchunks, chunk, 0)


def _compact(m_blk, m_t, a_t, pp_t, cap):
    ne, nb, _ = m_blk.shape
    assert cap % LANES == 0
    nchunks = cap // LANES
    per_e = lambda s: pl.BlockSpec((1,) + s, lambda e: (e, 0, 0))
    return pl.pallas_call(
        _compact_kernel,
        grid=(ne,),
        in_specs=[per_e((nb, LANES)), per_e((LANES, nb)), per_e((LANES, nb)), per_e((LANES, nb))],
        out_specs=[per_e((nchunks, LANES))] * 3,
        out_shape=[
            jax.ShapeDtypeStruct((ne, nchunks, LANES), I32),
            jax.ShapeDtypeStruct((ne, nchunks, LANES), F32),
            jax.ShapeDtypeStruct((ne, nchunks, LANES), I32),
        ],
        compiler_params=pltpu.CompilerParams(
            dimension_semantics=("arbitrary",), vmem_limit_bytes=VMEM_LIMIT),
        name="compact",
    )(m_blk, m_t, a_t, pp_t)


def _ffn_kernel(idx_ref, idxn_ref, ppp_ref, ppc_ref, gate_ref, x_hbm, fng_ref, wg_ref, wu_ref, wd_ref,
                g_hbm, buf, gsem, ssem, *, tiles_per_expert):
    tm = buf.shape[1]
    step = pl.program_id(0) * tiles_per_expert + pl.program_id(1)
    nsteps = pl.num_programs(0) * tiles_per_expert
    slot = step % 2
    other = 1 - slot

    def gather_copy(row, r, s):
        return pltpu.make_async_copy(x_hbm.at[pl.ds(row, 1), :], buf.at[s, pl.ds(r, 1), :], gsem.at[s])

    def scatter_copy(row, r, s):
        return pltpu.make_async_copy(buf.at[2 + s, pl.ds(r, 1), :], g_hbm.at[pl.ds(row, 1), :], ssem.at[s])

    def wait_gather(s):
        pltpu.make_async_copy(x_hbm.at[pl.ds(0, tm), :], buf.at[s], gsem.at[s]).wait()

    def wait_scatter(s):
        pltpu.make_async_copy(buf.at[2 + s], g_hbm.at[pl.ds(0, tm), :], ssem.at[s]).wait()

    @pl.when(step == 0)
    def _():
        def body(r, carry):
            gather_copy(idx_ref[0, 0, r], r, 0).start()
            return carry
        lax.fori_loop(0, tm, body, 0, unroll=8)
        buf[3] = jnp.zeros((tm, D_MODEL), F32)

    @pl.when(step > 0)
    def _():
        wait_scatter(slot)

    wait_gather(slot)

    xt = buf[slot]
    yslot = 2 + slot
    xn = (xt * lax.rsqrt(jnp.mean(xt * xt, axis=-1, keepdims=True) + EPS) * fng_ref[...]).astype(BF16)

    ri = lax.broadcasted_iota(I32, (tm, tm), 0)
    ci = lax.broadcasted_iota(I32, (tm, tm), 1)
    eye = jnp.where(ri == ci, 1.0, 0.0).astype(BF16)
    g1, g2, g3 = _split3(jnp.broadcast_to(gate_ref[0], (8, tm)))
    nt = lambda g: lax.dot_general(eye, g, _NT, preferred_element_type=F32)
    gcol = ((nt(g1) + nt(g2)) + nt(g3))[:, 0:1]

    nchunk = EXPERT_FF // FFN_CHUNK
    per = tm // nchunk
    for c in range(nchunk):
        for r in range(c * per, (c + 1) * per):
            gather_copy(idxn_ref[0, 0, r], r, other).start()
            scatter_copy(ppp_ref[0, 0, r], r, other).start()
        cols = slice(c * FFN_CHUNK, (c + 1) * FFN_CHUNK)
        hg = jnp.dot(xn, wg_ref[0, :, cols], preferred_element_type=F32)
        hu = jnp.dot(xn, wu_ref[0, :, cols], preferred_element_type=F32)
        hh = (jax.nn.silu(hg) * hu).astype(BF16)
        part = jnp.dot(hh, wd_ref[0, cols, :], preferred_element_type=F32)
        if c == 0:
            buf[yslot] = part
        elif c < nchunk - 1:
            buf[yslot] = buf[yslot] + part
        else:
            buf[yslot] = (buf[yslot] + part) * gcol

    @pl.when(step + 1 == nsteps)
    def _():
        def body(r, carry):
            scatter_copy(ppc_ref[0, 0, r], r, slot).start()
            return carry
        lax.fori_loop(0, tm, body, 0, unroll=8)
        wait_scatter(other)
        wait_scatter(slot)
        wait_gather(other)


def _ffn(idx_t, gate_t, pp_t, x1, fng, wg, wu, wd, n_pairs):
    ntiles, _, tm = idx_t.shape
    tpe = ntiles // N_EXPERTS
    assert tm % (EXPERT_FF // FFN_CHUNK) == 0
    cur = lambda e, i: (e * tpe + i, 0, 0)
    nxt = lambda e, i: (jnp.minimum(e * tpe + i + 1, ntiles - 1), 0, 0)
    prv = lambda e, i: (e * tpe + i, 0, 0)
    cur1 = lambda e, i: (e * tpe + i + 1, 0, 0)
    wmap = lambda e, i: (e, 0, 0)
    spare = (n_pairs + jnp.arange(tm, dtype=I32)).reshape(1, 1, tm)
    pp_ext = jnp.concatenate([spare, pp_t], axis=0)
    kernel = functools.partial(_ffn_kernel, tiles_per_expert=tpe)
    return pl.pallas_call(
        kernel,
        grid=(N_EXPERTS, tpe),
        in_specs=[
            pl.BlockSpec((1, 1, tm), cur, memory_space=pltpu.SMEM),
            pl.BlockSpec((1, 1, tm), nxt, memory_space=pltpu.SMEM),
            pl.BlockSpec((1, 1, tm), prv, memory_space=pltpu.SMEM),
            pl.BlockSpec((1, 1, tm), cur1, memory_space=pltpu.SMEM),
            pl.BlockSpec((1, 1, tm), cur),
            pl.BlockSpec(memory_space=pl.ANY),
            _const_spec((1, D_MODEL)),
            pl.BlockSpec((1, D_MODEL, EXPERT_FF), wmap),
            pl.BlockSpec((1, D_MODEL, EXPERT_FF), wmap),
            pl.BlockSpec((1, EXPERT_FF, D_MODEL), wmap),
        ],
        out_specs=pl.BlockSpec(memory_space=pl.ANY),
        out_shape=jax.ShapeDtypeStruct((n_pairs + tm, D_MODEL), F32),
        scratch_shapes=[
            pltpu.VMEM((4, tm, D_MODEL), F32),
            pltpu.SemaphoreType.DMA((2,)),
            pltpu.SemaphoreType.DMA((2,)),
        ],
        compiler_params=pltpu.CompilerParams(
            dimension_semantics=("arbitrary", "arbitrary"), vmem_limit_bytes=VMEM_LIMIT),
        name="ffn",
    )(idx_t, idx_t, pp_ext, pp_ext, gate_t, x1, fng, wg, wu, wd)


def _combine_kernel(boff_ref, x1_ref, ctok_ref, cnt_ref, g_hbm, y_ref, gbuf, acc_ref, sem, st_ref, *, n_pairs):
    i = pl.program_id(0)
    tb = x1_ref.shape[0]
    nslot, pc, _ = gbuf.shape
    n_chunks = n_pairs // pc
    r = tb // LANES

    def chunk_copy(g):
        s = g % nslot
        return pltpu.make_async_copy(g_hbm.at[pl.ds(g * pc, pc), :], gbuf.at[s], sem.at[s])

    @pl.when(i == 0)
    def _():
        st_ref[0] = 0
        st_ref[1] = 0

    g_lo = boff_ref[i] // pc
    g_hi = (boff_ref[i + 1] + pc - 1) // pc

    ri = lax.broadcasted_iota(I32, (LANES, LANES), 0)
    ci = lax.broadcasted_iota(I32, (LANES, LANES), 1)
    eye = jnp.where(ri == ci, 1.0, 0.0).astype(BF16)

    def to_col(rowv):
        p1, p2, p3 = _split3(jnp.broadcast_to(rowv, (8, LANES)))
        nt = lambda p: lax.dot_general(eye, p, _NT, preferred_element_type=F32)
        return ((nt(p1) + nt(p2)) + nt(p3))[:, 0:1]

    cex_rows = ctok_ref[pl.ds(i * r, r), :]
    cnt_rows = cnt_ref[pl.ds(i * r, r), :]
    cex = jnp.concatenate([to_col(cex_rows[k:k + 1]) for k in range(r)], axis=0)
    cin = cex + jnp.concatenate([to_col(cnt_rows[k:k + 1]) for k in range(r)], axis=0)

    acc_ref[...] = jnp.zeros_like(acc_ref)

    def body(g, carry):
        def start_one(k, c2):
            chunk_copy(k).start()
            return c2
        upto = jnp.minimum(g + nslot, n_chunks)
        lax.fori_loop(st_ref[0], upto, start_one, 0)
        st_ref[0] = jnp.maximum(st_ref[0], upto)

        @pl.when(g >= st_ref[1])
        def _():
            chunk_copy(g).wait()
            st_ref[1] = g + 1

        p = (g * pc + lax.broadcasted_iota(I32, (1, pc), 1)).astype(F32)
        sel = (cex <= p) & (p < cin)
        acc_ref[...] += jnp.dot(jnp.where(sel, 1.0, 0.0).astype(BF16), gbuf[g % nslot].astype(BF16),
                                preferred_element_type=F32)
        return carry

    lax.fori_loop(g_lo, g_hi, body, 0)
    y_ref[...] = x1_ref[...] + acc_ref[...]


def _combine(boff, x1, ctok, cnt, g, n_pairs):
    t = x1.shape[0]
    tb = COMB_TOKENS
    nb = ctok.shape[0]
    assert t % tb == 0 and n_pairs % COMB_PAIRS == 0
    kernel = functools.partial(_combine_kernel, n_pairs=n_pairs)
    grid_spec = pltpu.PrefetchScalarGridSpec(
        num_scalar_prefetch=1,
        grid=(t // tb,),
        in_specs=[
            pl.BlockSpec((tb, D_MODEL), lambda i, b: (i, 0)),
            pl.BlockSpec((nb, LANES), lambda i, b: (0, 0)),
            pl.BlockSpec((nb, LANES), lambda i, b: (0, 0)),
            pl.BlockSpec(memory_space=pl.ANY),
        ],
        out_specs=pl.BlockSpec((tb, D_MODEL), lambda i, b: (i, 0)),
        scratch_shapes=[
            pltpu.VMEM((COMB_SLOTS, COMB_PAIRS, D_MODEL), F32),
            pltpu.VMEM((tb, D_MODEL), F32),
            pltpu.SemaphoreType.DMA((COMB_SLOTS,)),
            pltpu.SMEM((2,), I32),
        ],
    )
    return pl.pallas_call(
        kernel,
        grid_spec=grid_spec,
        out_shape=jax.ShapeDtypeStruct((t, D_MODEL), F32),
        compiler_params=pltpu.CompilerParams(
            dimension_semantics=("arbitrary",), vmem_limit_bytes=VMEM_LIMIT),
        name="combine",
    )(boff, x1, ctok, cnt, g)


def _alibi_bias():
    slopes = 2.0 ** (-8.0 * jnp.arange(1, N_Q_HEADS + 1, dtype=F32) / N_Q_HEADS)
    qi = jnp.arange(CHUNK)[:, None]
    kj = jnp.arange(3 * CHUNK)[None, :]
    rel = jnp.abs(qi + CHUNK - kj)
    bias = -slopes[:, None, None] * rel.astype(F32)[None] * LOG2E
    return jnp.where((rel <= CHUNK)[None], bias, NEG_INF)


def _prep_weights(attn_norm_g, w_in, q_norm_g, k_norm_g, sink, gm_norm_g, w_spatial, b_spatial,
                  attn_out_g, gm_out_g, w_out, ffn_norm_g, w_router, w_gate, w_up, w_down):
    nqkv = ATTN_WIDTH + 2 * KV_WIDTH
    ang = attn_norm_g.reshape(1, D_MODEL)
    wqkv = w_in[:, :nqkv].astype(BF16)
    wugt = w_in[:, nqkv:].T.astype(BF16)
    qg = jnp.tile(q_norm_g, 2).reshape(1, LANES)
    kg = jnp.tile(k_norm_g, 2).reshape(1, LANES)
    gng = gm_norm_g.reshape(GM_GROUPS, HEAD_DIM, 1)
    wst = jnp.swapaxes(w_spatial, 1, 2).astype(BF16)
    zero = jnp.zeros((GM_GROUPS // 2, CHUNK, CHUNK), BF16)
    wsp = jnp.concatenate([jnp.concatenate([wst[0::2], zero], axis=2),
                           jnp.concatenate([zero, wst[1::2]], axis=2)], axis=1)
    bsp = b_spatial.reshape(GM_GROUPS, 1, CHUNK)
    aog = attn_out_g.reshape(1, ATTN_WIDTH)
    gog = gm_out_g.reshape(1, GM_WIDTH)
    wout = w_out.astype(BF16)
    fng = ffn_norm_g.reshape(1, D_MODEL)
    wr_t = w_router.T
    wr_hi = wr_t.astype(BF16)
    wr_lo = (wr_t - wr_hi.astype(F32)).astype(BF16)
    wrs = jnp.concatenate([wr_hi, wr_lo], axis=0)
    mixer_consts = (_alibi_bias(), sink * LOG2E, wsp, bsp, aog, gog, wout, fng, wrs, wr_hi)
    in_consts = (ang, wqkv, wugt, qg, kg, gng)
    ffn_consts = (fng, w_gate.astype(BF16), w_up.astype(BF16), w_down.astype(BF16))
    return in_consts, mixer_consts, ffn_consts


def _layer(x, in_consts, mixer_consts, ffn_consts):
    b, s, d = x.shape
    t = b * s
    x2d = x.reshape(t, d)
    q, k, v, ut, gt = _in_proj(x2d, *in_consts)
    x1, aff_t = _mixer(q, k, v, ut, gt, x2d, s, mixer_consts)

    nb = t // LANES
    cap = max(1, CAPACITY_FACTOR * t // N_EXPERTS)
    tok_bits = max(1, (t - 1).bit_length())
    a_blk = aff_t.reshape(N_EXPERTS, nb, LANES)
    a_t = jnp.swapaxes(a_blk, 1, 2)
    m_blk, m_t, pp_t, ctok, cnt = _select(a_blk, a_t, cap, tok_bits)
    idx, gate, pp = _compact(m_blk, m_t, a_t, pp_t, cap)

    tm = min(FFN_ROWS, cap)
    ntiles = N_EXPERTS * cap // tm
    n_pairs = N_EXPERTS * cap
    g = _ffn(idx.reshape(ntiles, 1, tm), gate.reshape(ntiles, 1, tm), pp.reshape(ntiles, 1, tm),
             x1, *ffn_consts, n_pairs)

    boff = jnp.concatenate([ctok.reshape(-1)[::COMB_TOKENS].astype(I32), jnp.full((1,), n_pairs, I32)])
    y = _combine(boff, x1, ctok, cnt, g, n_pairs)
    return y.reshape(b, s, d)


def kernel(x_prompt, x_sample, attn_norm_g, w_in, q_norm_g, k_norm_g, sink, gm_norm_g, w_spatial, b_spatial, attn_out_g, gm_out_g, w_out, ffn_norm_g, w_router, w_gate, w_up, w_down):
    y_prompt, y_sample = x_prompt, x_sample
    for l in range(w_in.shape[0]):
        consts = _prep_weights(attn_norm_g[l], w_in[l], q_norm_g[l], k_norm_g[l], sink[l], gm_norm_g[l],
                               w_spatial[l], b_spatial[l], attn_out_g[l], gm_out_g[l], w_out[l], ffn_norm_g[l],
                               w_router[l], w_gate[l], w_up[l], w_down[l])
        y_prompt = _layer(y_prompt, *consts)
        y_sample = _layer(y_sample, *consts)
    return (y_prompt, y_sample)
```

```python
import functools

import jax
import jax.numpy as jnp
from jax import lax
from jax.experimental import pallas as pl
from jax.experimental.pallas import tpu as pltpu

F32 = jnp.float32
BF16 = jnp.bfloat16
I32 = jnp.int32

D_MODEL = 2048
HEAD_DIM = 64
ATTN_WIDTH = 1024
KV_WIDTH = 256
GM_WIDTH = 1024
N_Q_HEADS = 16
N_KV_HEADS = 4
GQA_GROUP = 4
GM_GROUPS = 16
CHUNK = 128
N_EXPERTS = 16
EXPERT_FF = 1024
CAPACITY_FACTOR = 2
EPS = 1e-6
NEG_INF = -1e30
LOG2E = 1.4426950408889634
Q_SCALE = HEAD_DIM ** -0.5 * LOG2E

LANES = 128
VMEM_LIMIT = 58 * 1024 * 1024

IN_TOKENS = 512
MIX_TOKENS = 512
FFN_ROWS = 512
FFN_CHUNK = 256
COMB_TOKENS = 512
COMB_PAIRS = 256
COMB_SLOTS = 4

_NT = (((1,), (1,)), ((), ()))


def _const_spec(shape):
    nd = len(shape)
    return pl.BlockSpec(shape, lambda *_: (0,) * nd, pipeline_mode=pl.Buffered(1))


def _split3(a):
    a1 = a.astype(BF16)
    r1 = a - a1.astype(F32)
    a2 = r1.astype(BF16)
    a3 = (r1 - a2.astype(F32)).astype(BF16)
    return a1, a2, a3


def _exact_dot(a, b01):
    a1, a2, a3 = _split3(a)
    d = lambda x: jnp.dot(x, b01, preferred_element_type=F32)
    return (d(a1) + d(a2)) + d(a3)


def _exact_dot_left(a01, b):
    b1, b2, b3 = _split3(b)
    d = lambda x: jnp.dot(a01, x, preferred_element_type=F32)
    return (d(b1) + d(b2)) + d(b3)


def _gelu(x):
    return jax.nn.gelu(x)


def _in_proj_kernel(x_ref, ang_ref, wqkv_ref, wugt_ref, qg_ref, kg_ref, gng_ref,
                    q_ref, k_ref, v_ref, ut_ref, gt_ref):
    xf = x_ref[...]
    ms = jnp.mean(xf * xf, axis=-1, keepdims=True)
    h = (xf * lax.rsqrt(ms + EPS) * ang_ref[...]).astype(BF16)
    tm = xf.shape[0]

    zu = lax.dot_general(wugt_ref[:GM_WIDTH, :], h, _NT, preferred_element_type=F32)
    zg = lax.dot_general(wugt_ref[GM_WIDTH:, :], h, _NT, preferred_element_type=F32)
    ut_ref[...] = _gelu(zu)
    z = jnp.dot(h, wqkv_ref[...], preferred_element_type=F32)
    gl3 = _gelu(zg).reshape(GM_GROUPS, HEAD_DIM, tm)
    msg = jnp.mean(gl3 * gl3, axis=1, keepdims=True)
    gn = gl3 * lax.rsqrt(msg + EPS) * gng_ref[...]
    gt_ref[...] = gn.reshape(GM_WIDTH, tm).astype(BF16)

    low = lax.broadcasted_iota(I32, (1, LANES), 1) < HEAD_DIM

    def two_head_norm(zt, g):
        sq = zt * zt
        s_lo = jnp.sum(jnp.where(low, sq, 0.0), axis=-1, keepdims=True)
        s_hi = jnp.sum(jnp.where(low, 0.0, sq), axis=-1, keepdims=True)
        r = jnp.where(low, lax.rsqrt(s_lo * (1.0 / HEAD_DIM) + EPS), lax.rsqrt(s_hi * (1.0 / HEAD_DIM) + EPS))
        return zt * r * g

    for j in range(ATTN_WIDTH // LANES):
        qn = two_head_norm(z[:, j * LANES:(j + 1) * LANES], qg_ref[...])
        q_ref[:, j * LANES:(j + 1) * LANES] = (qn * Q_SCALE).astype(BF16)
    for j in range(KV_WIDTH // LANES):
        c0 = ATTN_WIDTH + j * LANES
        k_ref[:, j * LANES:(j + 1) * LANES] = two_head_norm(z[:, c0:c0 + LANES], kg_ref[...]).astype(BF16)
    v_ref[...] = z[:, ATTN_WIDTH + KV_WIDTH:].astype(BF16)


def _in_proj(x2d, ang, wqkv, wugt, qg, kg, gng):
    t = x2d.shape[0]
    tm = IN_TOKENS
    assert t % tm == 0
    row = lambda i: (i, 0)
    col = lambda i: (0, i)
    return pl.pallas_call(
        _in_proj_kernel,
        grid=(t // tm,),
        in_specs=[
            pl.BlockSpec((tm, D_MODEL), row),
            _const_spec((1, D_MODEL)),
            _const_spec((D_MODEL, ATTN_WIDTH + 2 * KV_WIDTH)),
            _const_spec((2 * GM_WIDTH, D_MODEL)),
            _const_spec((1, LANES)),
            _const_spec((1, LANES)),
            _const_spec((GM_GROUPS, HEAD_DIM, 1)),
        ],
        out_specs=[
            pl.BlockSpec((tm, ATTN_WIDTH), row),
            pl.BlockSpec((tm, KV_WIDTH), row),
            pl.BlockSpec((tm, KV_WIDTH), row),
            pl.BlockSpec((GM_WIDTH, tm), col),
            pl.BlockSpec((GM_WIDTH, tm), col),
        ],
        out_shape=[
            jax.ShapeDtypeStruct((t, ATTN_WIDTH), BF16),
            jax.ShapeDtypeStruct((t, KV_WIDTH), BF16),
            jax.ShapeDtypeStruct((t, KV_WIDTH), BF16),
            jax.ShapeDtypeStruct((GM_WIDTH, t), F32),
            jax.ShapeDtypeStruct((GM_WIDTH, t), BF16),
        ],
        compiler_params=pltpu.CompilerParams(
            dimension_semantics=("arbitrary",), vmem_limit_bytes=VMEM_LIMIT),
        name="in_proj",
    )(x2d, ang, wqkv, wugt, qg, kg, gng)


def _mixer_kernel(q_ref, kp_ref, kc_ref, kn_ref, vp_ref, vc_ref, vn_ref, ut_ref, gt_ref, x_ref,
                  bias_ref, sink_ref, wsp_ref, bsp_ref, aog_ref, gog_ref, wout_ref, fng_ref,
                  wrs_ref, wrh_ref, x1_ref, aff_ref, *, blocks_per_seq):
    i = pl.program_id(0)
    tb = q_ref.shape[0]
    nsub = tb // CHUNK
    first = (i % blocks_per_seq) == 0
    last = (i % blocks_per_seq) == blocks_per_seq - 1

    kcat = jnp.concatenate([kp_ref[...], kc_ref[...], kn_ref[...]], axis=0)
    vcat = jnp.concatenate([vp_ref[...], vc_ref[...], vn_ref[...]], axis=0)
    key = lax.broadcasted_iota(I32, (1, 3 * CHUNK), 1)
    low = lax.broadcasted_iota(I32, (1, LANES), 1) < HEAD_DIM
    zero = jnp.zeros((), BF16)
    ones_bd = jnp.where((lax.broadcasted_iota(I32, (6 * CHUNK, LANES), 0) < 3 * CHUNK)
                        == (lax.broadcasted_iota(I32, (6 * CHUNK, LANES), 1) < HEAD_DIM), 1.0, 0.0).astype(BF16)

    def swap_halves(t):
        return jnp.concatenate([t[:, HEAD_DIM:], t[:, :HEAD_DIM]], axis=1)

    attn_rows = []
    for sb in range(nsub):
        edge = None
        if sb == 0:
            edge = jnp.where((key < CHUNK) & first, NEG_INF, 0.0)
        if sb == nsub - 1:
            hi = jnp.where((key >= 2 * CHUNK) & last, NEG_INF, 0.0)
            edge = hi if edge is None else edge + hi
        tiles = []
        for jt in range(N_KV_HEADS // 2):
            kt = kcat[sb * CHUNK:sb * CHUNK + 3 * CHUNK, jt * LANES:(jt + 1) * LANES]
            vt = vcat[sb * CHUNK:sb * CHUNK + 3 * CHUNK, jt * LANES:(jt + 1) * LANES]
            kts, vts = swap_halves(kt), swap_halves(vt)
            for jj in range(2):
                j = 2 * jt + jj
                ksame, kswap = (kt, kts) if jj == 0 else (kts, kt)
                vsame, vswap = (vt, vts) if jj == 0 else (vts, vt)
                krhs = jnp.concatenate([jnp.where(low, ksame, zero), jnp.where(low, zero, kswap)], axis=0)
                vrhs = jnp.concatenate([jnp.where(low, vsame, zero), jnp.where(low, zero, vswap)], axis=0)
                vaug = jnp.concatenate([vrhs, ones_bd], axis=1)
                for pr in range(GQA_GROUP // 2):
                    ha = GQA_GROUP * j + 2 * pr
                    qp = q_ref[sb * CHUNK:(sb + 1) * CHUNK, ha * HEAD_DIM:(ha + 2) * HEAD_DIM]
                    s = lax.dot_general(qp, krhs, _NT, preferred_element_type=F32)
                    ps, es = [], []
                    for g in range(2):
                        sh = s[:, g * 3 * CHUNK:(g + 1) * 3 * CHUNK] + bias_ref[ha + g]
                        if edge is not None:
                            sh = sh + edge
                        snk = sink_ref[ha + g]
                        m = jnp.maximum(jnp.max(sh, axis=-1, keepdims=True), snk)
                        ps.append(jnp.exp2(sh - m).astype(BF16))
                        es.append(jnp.exp2(snk - m))
                    o = jnp.dot(jnp.concatenate(ps, axis=1), vaug, preferred_element_type=F32)
                    den = o[:, LANES:] + jnp.where(low, es[0], es[1])
                    tiles.append(o[:, :LANES] / den)
        attn_rows.append(jnp.concatenate(tiles, axis=1))
    attn = jnp.concatenate(attn_rows, axis=0)
    an = attn * lax.rsqrt(jnp.mean(attn * attn, axis=-1, keepdims=True) + EPS) * aog_ref[...]

    st_rows = []
    for pi in range(GM_GROUPS // 2):
        r0 = pi * 2 * HEAD_DIM
        lhs = jnp.concatenate(
            [jnp.concatenate([gt_ref[r0:r0 + HEAD_DIM, c * CHUNK:(c + 1) * CHUNK],
                              gt_ref[r0 + HEAD_DIM:r0 + 2 * HEAD_DIM, c * CHUNK:(c + 1) * CHUNK]], axis=1)
             for c in range(nsub)], axis=0)
        o = jnp.dot(lhs, wsp_ref[pi], preferred_element_type=F32)
        ga = jnp.concatenate([o[c * HEAD_DIM:(c + 1) * HEAD_DIM, :CHUNK] for c in range(nsub)], axis=1)
        gb = jnp.concatenate([o[c * HEAD_DIM:(c + 1) * HEAD_DIM, CHUNK:] for c in range(nsub)], axis=1)
        ba = jnp.concatenate([bsp_ref[2 * pi]] * nsub, axis=1)
        bb = jnp.concatenate([bsp_ref[2 * pi + 1]] * nsub, axis=1)
        st_rows.append(ga + ba)
        st_rows.append(gb + bb)
    gmt = ut_ref[...] * jnp.concatenate(st_rows, axis=0)
    gm = gmt.T
    gn = gm * lax.rsqrt(jnp.mean(gm * gm, axis=-1, keepdims=True) + EPS) * gog_ref[...]

    cat = jnp.concatenate([an.astype(BF16), gn.astype(BF16)], axis=1)
    x1 = x_ref[...] + jnp.dot(cat, wout_ref[...], preferred_element_type=F32)
    x1_ref[...] = x1

    xn = x1 * lax.rsqrt(jnp.mean(x1 * x1, axis=-1, keepdims=True) + EPS) * fng_ref[...]
    xh = xn.astype(BF16)
    xl = (xn - xh.astype(F32)).astype(BF16)
    r1 = lax.dot_general(wrs_ref[...], xh, _NT, preferred_element_type=F32)
    r2 = lax.dot_general(wrh_ref[...], xl, _NT, preferred_element_type=F32)
    logits = (r1[:N_EXPERTS] + r1[N_EXPERTS:]) + r2
    mx = jnp.max(logits, axis=0, keepdims=True)
    ex = jnp.exp(logits - mx)
    aff_ref[...] = ex / jnp.sum(ex, axis=0, keepdims=True)


def _mixer(q, k, v, ut, gt, x2d, seq_len, consts):
    t = q.shape[0]
    tb = MIX_TOKENS
    assert seq_len % tb == 0 and t % seq_len == 0
    r = tb // CHUNK
    nch = t // CHUNK
    row = lambda i: (i, 0)
    col = lambda i: (0, i)
    prev = lambda i: (jnp.maximum(i * r - 1, 0), 0)
    nxt = lambda i: (jnp.minimum((i + 1) * r, nch - 1), 0)
    (bias, sink, wsp, bsp, aog, gog, wout, fng, wrs, wrh) = consts
    kernel = functools.partial(_mixer_kernel, blocks_per_seq=seq_len // tb)
    return pl.pallas_call(
        kernel,
        grid=(t // tb,),
        in_specs=[
            pl.BlockSpec((tb, ATTN_WIDTH), row),
            pl.BlockSpec((CHUNK, KV_WIDTH), prev),
            pl.BlockSpec((tb, KV_WIDTH), row),
            pl.BlockSpec((CHUNK, KV_WIDTH), nxt),
            pl.BlockSpec((CHUNK, KV_WIDTH), prev),
            pl.BlockSpec((tb, KV_WIDTH), row),
            pl.BlockSpec((CHUNK, KV_WIDTH), nxt),
            pl.BlockSpec((GM_WIDTH, tb), col),
            pl.BlockSpec((GM_WIDTH, tb), col),
            pl.BlockSpec((tb, D_MODEL), row),
            _const_spec((N_Q_HEADS, CHUNK, 3 * CHUNK)),
            pl.BlockSpec(memory_space=pltpu.SMEM),
            _const_spec((GM_GROUPS // 2, 2 * CHUNK, 2 * CHUNK)),
            _const_spec((GM_GROUPS, 1, CHUNK)),
            _const_spec((1, ATTN_WIDTH)),
            _const_spec((1, GM_WIDTH)),
            _const_spec((D_MODEL, D_MODEL)),
            _const_spec((1, D_MODEL)),
            _const_spec((2 * N_EXPERTS, D_MODEL)),
            _const_spec((N_EXPERTS, D_MODEL)),
        ],
        out_specs=[
            pl.BlockSpec((tb, D_MODEL), row),
            pl.BlockSpec((N_EXPERTS, tb), col),
        ],
        out_shape=[
            jax.ShapeDtypeStruct((t, D_MODEL), F32),
            jax.ShapeDtypeStruct((N_EXPERTS, t), F32),
        ],
        compiler_params=pltpu.CompilerParams(
            dimension_semantics=("arbitrary",), vmem_limit_bytes=VMEM_LIMIT),
        name="mixer",
    )(q, k, k, k, v, v, v, ut, gt, x2d, bias, sink, wsp, bsp, aog, gog, wout, fng, wrs, wrh)


def _sum12(x):
    return jnp.sum(jnp.sum(x, axis=2, keepdims=True), axis=1, keepdims=True)


def _select_kernel(a_ref, at_ref, m_ref, mt_ref, ppt_ref, ctok_ref, cnt_ref, *, cap, tok_bits):
    ne, nb, _ = a_ref.shape
    a = a_ref[...]
    capf = jnp.float32(cap)

    def thr_step(it, thr):
        cand = thr | jnp.left_shift(jnp.int32(1), 30 - it)
        cnt = _sum12(jnp.where(a >= lax.bitcast_convert_type(cand, F32), 1.0, 0.0))
        return jnp.where(cnt >= capf, cand, thr)

    thr = lax.bitcast_convert_type(lax.fori_loop(0, 31, thr_step, jnp.zeros((ne, 1, 1), I32)), F32)

    gt = a > thr
    eq = a == thr
    need = capf - _sum12(jnp.where(gt, 1.0, 0.0))
    tok = lax.broadcasted_iota(I32, (1, nb, LANES), 1) * LANES + lax.broadcasted_iota(I32, (1, nb, LANES), 2)

    def cut_step(it, c):
        cand = c + jnp.left_shift(jnp.int32(1), tok_bits - 1 - it)
        f = _sum12(jnp.where(eq & (tok < cand), 1.0, 0.0))
        return jnp.where(f < need, cand, c)

    c = lax.fori_loop(0, tok_bits, cut_step, jnp.zeros((ne, 1, 1), I32))
    tcut = jnp.where(need > 0.0, c + 1, 0)

    m = jnp.where(gt | (eq & (tok < tcut)), 1.0, 0.0)
    m_ref[...] = m.astype(BF16)

    a_t = at_ref[...]
    tok_t = lax.broadcasted_iota(I32, (1, LANES, nb), 2) * LANES + lax.broadcasted_iota(I32, (1, LANES, nb), 1)
    mt = jnp.where((a_t > thr) | ((a_t == thr) & (tok_t < tcut)), 1.0, 0.0)
    mt_ref[...] = mt.astype(BF16)

    cnt_t = jnp.sum(mt, axis=0)
    rl = lax.broadcasted_iota(I32, (LANES, LANES), 0)
    cl = lax.broadcasted_iota(I32, (LANES, LANES), 1)
    tri_strict_low = jnp.where(cl < rl, 1.0, 0.0).astype(BF16)
    local_t = jnp.dot(tri_strict_low, cnt_t.astype(BF16), preferred_element_type=F32)
    tot_t = jnp.sum(cnt_t, axis=0, keepdims=True)
    rb = lax.broadcasted_iota(I32, (nb, nb), 0)
    cb = lax.broadcasted_iota(I32, (nb, nb), 1)
    blk_before = jnp.where(rb < cb, 1.0, 0.0).astype(BF16)
    pb_t = _exact_dot(jnp.broadcast_to(tot_t, (8, nb)), blk_before)[0:1]
    ctok_t = local_t + pb_t
    rank = jnp.zeros((LANES, nb), F32)
    for e in range(ne):
        ppt_ref[e] = ctok_t + rank
        rank = rank + mt[e]

    cnt_b = jnp.sum(m, axis=0)
    tri_strict_up = jnp.where(rl < cl, 1.0, 0.0).astype(BF16)
    local_b = jnp.dot(cnt_b.astype(BF16), tri_strict_up, preferred_element_type=F32)
    totb = jnp.dot(cnt_b.astype(BF16), jnp.ones((LANES, LANES), BF16), preferred_element_type=F32)
    blk_after = jnp.where(cb < rb, 1.0, 0.0).astype(BF16)
    ctok_ref[...] = local_b + _exact_dot_left(blk_after, totb)
    cnt_ref[...] = cnt_b


def _select(a_blk, a_t, cap, tok_bits):
    ne, nb, _ = a_blk.shape
    kernel = functools.partial(_select_kernel, cap=cap, tok_bits=tok_bits)
    full3 = lambda s: pl.BlockSpec(s, lambda: (0, 0, 0))
    full2 = lambda s: pl.BlockSpec(s, lambda: (0, 0))
    return pl.pallas_call(
        kernel,
        in_specs=[full3((ne, nb, LANES)), full3((ne, LANES, nb))],
        out_specs=[full3((ne, nb, LANES)), full3((ne, LANES, nb)), full3((ne, LANES, nb)),
                   full2((nb, LANES)), full2((nb, LANES))],
        out_shape=[
            jax.ShapeDtypeStruct((ne, nb, LANES), BF16),
            jax.ShapeDtypeStruct((ne, LANES, nb), BF16),
            jax.ShapeDtypeStruct((ne, LANES, nb), F32),
            jax.ShapeDtypeStruct((nb, LANES), F32),
            jax.ShapeDtypeStruct((nb, LANES), F32),
        ],
        compiler_params=pltpu.CompilerParams(vmem_limit_bytes=VMEM_LIMIT),
        name="select",
    )(a_blk, a_t)


def _compact_kernel(m_ref, mt_ref, at_ref, ppt_ref, idx_ref, gate_ref, pp_ref):
    nb = m_ref.shape[1]
    nchunks = idx_ref.shape[1]
    mb = m_ref[0]
    mt = mt_ref[0]
    a_t = at_ref[0]
    pp_t = ppt_ref[0]

    totb = jnp.dot(mb, jnp.ones((LANES, LANES), BF16), preferred_element_type=F32)
    rb = lax.broadcasted_iota(I32, (nb, nb), 0)
    cb = lax.broadcasted_iota(I32, (nb, nb), 1)
    blk_after = jnp.where(cb < rb, 1.0, 0.0).astype(BF16)
    cb_excl = jnp.dot(blk_after, totb.astype(BF16), preferred_element_type=F32)
    cb_incl = cb_excl + totb
    rl = lax.broadcasted_iota(I32, (LANES, LANES), 0)
    cl = lax.broadcasted_iota(I32, (LANES, LANES), 1)
    tri_low = jnp.where(cl <= rl, 1.0, 0.0).astype(BF16)
    lt = jnp.dot(tri_low, mt, preferred_element_type=F32).astype(BF16)
    sub_b = lax.broadcasted_iota(I32, (nb, LANES), 0).astype(F32)
    sub_l = rl.astype(F32)
    lane = lax.broadcasted_iota(I32, (1, LANES), 1).astype(F32)

    def chunk(c, carry):
        j = lane + lax.convert_element_type(c * LANES, F32)
        bj = jnp.sum(jnp.where(cb_incl <= j, 1.0, 0.0), axis=0, keepdims=True)
        oh = sub_b == bj
        ohb = jnp.where(oh, 1.0, 0.0).astype(BF16)
        jloc = j - jnp.sum(jnp.where(oh, cb_excl, 0.0), axis=0, keepdims=True)
        glt = jnp.dot(lt, ohb, preferred_element_type=F32)
        lj = jnp.sum(jnp.where(glt <= jloc, 1.0, 0.0), axis=0, keepdims=True)
        ohl = sub_l == lj
        gate = jnp.sum(jnp.where(ohl, _exact_dot(a_t, ohb), 0.0), axis=0, keepdims=True)
        pp = jnp.sum(jnp.where(ohl, _exact_dot(pp_t, ohb), 0.0), axis=0, keepdims=True)
        idx_ref[0, pl.ds(c, 1), :] = (bj * LANES + lj).astype(I32)
        gate_ref[0, pl.ds(c, 1), :] = gate
        pp_ref[0, pl.ds(c, 1), :] = pp.astype(I32)
        return carry

    lax.fori_loop(0, nchunks, chunk, 0, unroll=min(4, nchunks))


def _compact(m_blk, m_t, a_t, pp_t, cap):
    ne, nb, _ = m_blk.shape
    assert cap % LANES == 0
    nchunks = cap // LANES
    per_e = lambda s: pl.BlockSpec((1,) + s, lambda e: (e, 0, 0))
    return pl.pallas_call(
        _compact_kernel,
        grid=(ne,),
        in_specs=[per_e((nb, LANES)), per_e((LANES, nb)), per_e((LANES, nb)), per_e((LANES, nb))],
        out_specs=[per_e((nchunks, LANES))] * 3,
        out_shape=[
            jax.ShapeDtypeStruct((ne, nchunks, LANES), I32),
            jax.ShapeDtypeStruct((ne, nchunks, LANES), F32),
            jax.ShapeDtypeStruct((ne, nchunks, LANES), I32),
        ],
        compiler_params=pltpu.CompilerParams(
            dimension_semantics=("arbitrary",), vmem_limit_bytes=VMEM_LIMIT),
        name="compact",
    )(m_blk, m_t, a_t, pp_t)


def _ffn_kernel(idx_ref, idxn_ref, ppp_ref, ppc_ref, gate_ref, x_hbm, fng_ref, wg_ref, wu_ref, wd_ref,
                g_hbm, buf, gsem, ssem, *, tiles_per_expert):
    tm = buf.shape[1]
    step = pl.program_id(0) * tiles_per_expert + pl.program_id(1)
    nsteps = pl.num_programs(0) * tiles_per_expert
    slot = step % 2
    other = 1 - slot

    def gather_copy(row, r, s):
        return pltpu.make_async_copy(x_hbm.at[pl.ds(row, 1), :], buf.at[s, pl.ds(r, 1), :], gsem.at[s])

    def scatter_copy(row, r, s):
        return pltpu.make_async_copy(buf.at[2 + s, pl.ds(r, 1), :], g_hbm.at[pl.ds(row, 1), :], ssem.at[s])

    def wait_gather(s):
        pltpu.make_async_copy(x_hbm.at[pl.ds(0, tm), :], buf.at[s], gsem.at[s]).wait()

    def wait_scatter(s):
        pltpu.make_async_copy(buf.at[2 + s], g_hbm.at[pl.ds(0, tm), :], ssem.at[s]).wait()

    @pl.when(step == 0)
    def _():
        def body(r, carry):
            gather_copy(idx_ref[0, 0, r], r, 0).start()
            return carry
        lax.fori_loop(0, tm, body, 0, unroll=8)
        buf[3] = jnp.zeros((tm, D_MODEL), F32)

    def tile(ps):
        po = 1 - ps

        @pl.when(step > 0)
        def _():
            wait_scatter(ps)

        wait_gather(ps)

        xt = buf[slot]
        yslot = 2 + slot
        xn = (xt * lax.rsqrt(jnp.mean(xt * xt, axis=-1, keepdims=True) + EPS) * fng_ref[...]).astype(BF16)

        ri = lax.broadcasted_iota(I32, (tm, tm), 0)
        ci = lax.broadcasted_iota(I32, (tm, tm), 1)
        eye = jnp.where(ri == ci, 1.0, 0.0).astype(BF16)
        g1, g2, g3 = _split3(jnp.broadcast_to(gate_ref[0], (8, tm)))
        nt = lambda g: lax.dot_general(eye, g, _NT, preferred_element_type=F32)
        gcol = ((nt(g1) + nt(g2)) + nt(g3))[:, 0:1]

        nchunk = EXPERT_FF // FFN_CHUNK
        per = tm // nchunk
        for c in range(nchunk):
            for r in range(c * per, (c + 1) * per):
                gather_copy(idxn_ref[0, 0, r], r, po).start()
                scatter_copy(ppp_ref[0, 0, r], r, po).start()
            cols = slice(c * FFN_CHUNK, (c + 1) * FFN_CHUNK)
            hg = jnp.dot(xn, wg_ref[0, :, cols], preferred_element_type=F32)
            hu = jnp.dot(xn, wu_ref[0, :, cols], preferred_element_type=F32)
            hh = (jax.nn.silu(hg) * hu).astype(BF16)
            part = jnp.dot(hh, wd_ref[0, cols, :], preferred_element_type=F32)
            if c == 0:
                buf[yslot] = part
            elif c < nchunk - 1:
                buf[yslot] = buf[yslot] + part
            else:
                buf[yslot] = (buf[yslot] + part) * gcol

    for ps in range(2):
        pl.when(slot == ps)(functools.partial(tile, ps))

    @pl.when(step + 1 == nsteps)
    def _():
        def body(r, carry):
            scatter_copy(ppc_ref[0, 0, r], r, slot).start()
            return carry
        lax.fori_loop(0, tm, body, 0, unroll=8)
        wait_scatter(other)
        wait_scatter(slot)
        wait_gather(other)


def _ffn(idx_t, gate_t, pp_t, x1, fng, wg, wu, wd, n_pairs):
    ntiles, _, tm = idx_t.shape
    tpe = ntiles // N_EXPERTS
    assert tm % (EXPERT_FF // FFN_CHUNK) == 0
    cur = lambda e, i: (e * tpe + i, 0, 0)
    nxt = lambda e, i: (jnp.minimum(e * tpe + i + 1, ntiles - 1), 0, 0)
    prv = lambda e, i: (e * tpe + i, 0, 0)
    cur1 = lambda e, i: (e * tpe + i + 1, 0, 0)
    wmap = lambda e, i: (e, 0, 0)
    spare = (n_pairs + jnp.arange(tm, dtype=I32)).reshape(1, 1, tm)
    pp_ext = jnp.concatenate([spare, pp_t], axis=0)
    kernel = functools.partial(_ffn_kernel, tiles_per_expert=tpe)
    return pl.pallas_call(
        kernel,
        grid=(N_EXPERTS, tpe),
        in_specs=[
            pl.BlockSpec((1, 1, tm), cur, memory_space=pltpu.SMEM),
            pl.BlockSpec((1, 1, tm), nxt, memory_space=pltpu.SMEM),
            pl.BlockSpec((1, 1, tm), prv, memory_space=pltpu.SMEM),
            pl.BlockSpec((1, 1, tm), cur1, memory_space=pltpu.SMEM),
            pl.BlockSpec((1, 1, tm), cur),
            pl.BlockSpec(memory_space=pl.ANY),
            _const_spec((1, D_MODEL)),
            pl.BlockSpec((1, D_MODEL, EXPERT_FF), wmap),
            pl.BlockSpec((1, D_MODEL, EXPERT_FF), wmap),
            pl.BlockSpec((1, EXPERT_FF, D_MODEL), wmap),
        ],
        out_specs=pl.BlockSpec(memory_space=pl.ANY),
        out_shape=jax.ShapeDtypeStruct((n_pairs + tm, D_MODEL), F32),
        scratch_shapes=[
            pltpu.VMEM((4, tm, D_MODEL), F32),
            pltpu.SemaphoreType.DMA((2,)),
            pltpu.SemaphoreType.DMA((2,)),
        ],
        compiler_params=pltpu.CompilerParams(
            dimension_semantics=("arbitrary", "arbitrary"), vmem_limit_bytes=VMEM_LIMIT),
        name="ffn",
    )(idx_t, idx_t, pp_ext, pp_ext, gate_t, x1, fng, wg, wu, wd)


def _combine_kernel(boff_ref, x1_ref, ctok_ref, cnt_ref, g_hbm, y_ref, gbuf, acc_ref, sem, st_ref, *, n_pairs):
    i = pl.program_id(0)
    tb = x1_ref.shape[0]
    nslot, pc, _ = gbuf.shape
    n_chunks = n_pairs // pc
    r = tb // LANES

    def chunk_copy(g):
        s = g % nslot
        return pltpu.make_async_copy(g_hbm.at[pl.ds(g * pc, pc), :], gbuf.at[s], sem.at[s])

    @pl.when(i == 0)
    def _():
        st_ref[0] = 0
        st_ref[1] = 0

    g_lo = boff_ref[i] // pc
    g_hi = (boff_ref[i + 1] + pc - 1) // pc

    ri = lax.broadcasted_iota(I32, (LANES, LANES), 0)
    ci = lax.broadcasted_iota(I32, (LANES, LANES), 1)
    eye = jnp.where(ri == ci, 1.0, 0.0).astype(BF16)

    def to_col(rowv):
        p1, p2, p3 = _split3(jnp.broadcast_to(rowv, (8, LANES)))
        nt = lambda p: lax.dot_general(eye, p, _NT, preferred_element_type=F32)
        return ((nt(p1) + nt(p2)) + nt(p3))[:, 0:1]

    cex_rows = ctok_ref[pl.ds(i * r, r), :]
    cnt_rows = cnt_ref[pl.ds(i * r, r), :]
    cex = jnp.concatenate([to_col(cex_rows[k:k + 1]) for k in range(r)], axis=0)
    cin = cex + jnp.concatenate([to_col(cnt_rows[k:k + 1]) for k in range(r)], axis=0)

    acc_ref[...] = jnp.zeros_like(acc_ref)

    def body(g, carry):
        def start_one(k, c2):
            chunk_copy(k).start()
            return c2
        upto = jnp.minimum(g + nslot, n_chunks)
        lax.fori_loop(st_ref[0], upto, start_one, 0)
        st_ref[0] = jnp.maximum(st_ref[0], upto)

        @pl.when(g >= st_ref[1])
        def _():
            chunk_copy(g).wait()
            st_ref[1] = g + 1

        p = (g * pc + lax.broadcasted_iota(I32, (1, pc), 1)).astype(F32)
        sel = (cex <= p) & (p < cin)
        acc_ref[...] += jnp.dot(jnp.where(sel, 1.0, 0.0).astype(BF16), gbuf[g % nslot].astype(BF16),
                                preferred_element_type=F32)
        return carry

    lax.fori_loop(g_lo, g_hi, body, 0)
    y_ref[...] = x1_ref[...] + acc_ref[...]


def _combine(boff, x1, ctok, cnt, g, n_pairs):
    t = x1.shape[0]
    tb = COMB_TOKENS
    nb = ctok.shape[0]
    assert t % tb == 0 and n_pairs % COMB_PAIRS == 0
    kernel = functools.partial(_combine_kernel, n_pairs=n_pairs)
    grid_spec = pltpu.PrefetchScalarGridSpec(
        num_scalar_prefetch=1,
        grid=(t // tb,),
        in_specs=[
            pl.BlockSpec((tb, D_MODEL), lambda i, b: (i, 0)),
            pl.BlockSpec((nb, LANES), lambda i, b: (0, 0)),
            pl.BlockSpec((nb, LANES), lambda i, b: (0, 0)),
            pl.BlockSpec(memory_space=pl.ANY),
        ],
        out_specs=pl.BlockSpec((tb, D_MODEL), lambda i, b: (i, 0)),
        scratch_shapes=[
            pltpu.VMEM((COMB_SLOTS, COMB_PAIRS, D_MODEL), F32),
            pltpu.VMEM((tb, D_MODEL), F32),
            pltpu.SemaphoreType.DMA((COMB_SLOTS,)),
            pltpu.SMEM((2,), I32),
        ],
    )
    return pl.pallas_call(
        kernel,
        grid_spec=grid_spec,
        out_shape=jax.ShapeDtypeStruct((t, D_MODEL), F32),
        compiler_params=pltpu.CompilerParams(
            dimension_semantics=("arbitrary",), vmem_limit_bytes=VMEM_LIMIT),
        name="combine",
    )(boff, x1, ctok, cnt, g)


def _alibi_bias():
    slopes = 2.0 ** (-8.0 * jnp.arange(1, N_Q_HEADS + 1, dtype=F32) / N_Q_HEADS)
    qi = jnp.arange(CHUNK)[:, None]
    kj = jnp.arange(3 * CHUNK)[None, :]
    rel = jnp.abs(qi + CHUNK - kj)
    bias = -slopes[:, None, None] * rel.astype(F32)[None] * LOG2E
    return jnp.where((rel <= CHUNK)[None], bias, NEG_INF)


def _prep_weights(attn_norm_g, w_in, q_norm_g, k_norm_g, sink, gm_norm_g, w_spatial, b_spatial,
                  attn_out_g, gm_out_g, w_out, ffn_norm_g, w_router, w_gate, w_up, w_down):
    nqkv = ATTN_WIDTH + 2 * KV_WIDTH
    ang = attn_norm_g.reshape(1, D_MODEL)
    wqkv = w_in[:, :nqkv].astype(BF16)
    wugt = w_in[:, nqkv:].T.astype(BF16)
    qg = jnp.tile(q_norm_g, 2).reshape(1, LANES)
    kg = jnp.tile(k_norm_g, 2).reshape(1, LANES)
    gng = gm_norm_g.reshape(GM_GROUPS, HEAD_DIM, 1)
    wst = jnp.swapaxes(w_spatial, 1, 2).astype(BF16)
    zero = jnp.zeros((GM_GROUPS // 2, CHUNK, CHUNK), BF16)
    wsp = jnp.concatenate([jnp.concatenate([wst[0::2], zero], axis=2),
                           jnp.concatenate([zero, wst[1::2]], axis=2)], axis=1)
    bsp = b_spatial.reshape(GM_GROUPS, 1, CHUNK)
    aog = attn_out_g.reshape(1, ATTN_WIDTH)
    gog = gm_out_g.reshape(1, GM_WIDTH)
    wout = w_out.astype(BF16)
    fng = ffn_norm_g.reshape(1, D_MODEL)
    wr_t = w_router.T
    wr_hi = wr_t.astype(BF16)
    wr_lo = (wr_t - wr_hi.astype(F32)).astype(BF16)
    wrs = jnp.concatenate([wr_hi, wr_lo], axis=0)
    mixer_consts = (_alibi_bias(), sink * LOG2E, wsp, bsp, aog, gog, wout, fng, wrs, wr_hi)
    in_consts = (ang, wqkv, wugt, qg, kg, gng)
    ffn_consts = (fng, w_gate.astype(BF16), w_up.astype(BF16), w_down.astype(BF16))
    return in_consts, mixer_consts, ffn_consts


def _layer(x, in_consts, mixer_consts, ffn_consts):
    b, s, d = x.shape
    t = b * s
    x2d = x.reshape(t, d)
    q, k, v, ut, gt = _in_proj(x2d, *in_consts)
    x1, aff_t = _mixer(q, k, v, ut, gt, x2d, s, mixer_consts)

    nb = t // LANES
    cap = max(1, CAPACITY_FACTOR * t // N_EXPERTS)
    tok_bits = max(1, (t - 1).bit_length())
    a_blk = aff_t.reshape(N_EXPERTS, nb, LANES)
    a_t = jnp.swapaxes(a_blk, 1, 2)
    m_blk, m_t, pp_t, ctok, cnt = _select(a_blk, a_t, cap, tok_bits)
    idx, gate, pp = _compact(m_blk, m_t, a_t, pp_t, cap)

    tm = min(FFN_ROWS, cap)
    ntiles = N_EXPERTS * cap // tm
    n_pairs = N_EXPERTS * cap
    g = _ffn(idx.reshape(ntiles, 1, tm), gate.reshape(ntiles, 1, tm), pp.reshape(ntiles, 1, tm),
             x1, *ffn_consts, n_pairs)

    boff = jnp.concatenate([ctok.reshape(-1)[::COMB_TOKENS].astype(I32), jnp.full((1,), n_pairs, I32)])
    y = _combine(boff, x1, ctok, cnt, g, n_pairs)
    return y.reshape(b, s, d)


def kernel(x_prompt, x_sample, attn_norm_g, w_in, q_norm_g, k_norm_g, sink, gm_norm_g, w_spatial, b_spatial, attn_out_g, gm_out_g, w_out, ffn_norm_g, w_router, w_gate, w_up, w_down):
    y_prompt, y_sample = x_prompt, x_sample
    for l in range(w_in.shape[0]):
        consts = _prep_weights(attn_norm_g[l], w_in[l], q_norm_g[l], k_norm_g[l], sink[l], gm_norm_g[l],
                               w_spatial[l], b_spatial[l], attn_out_g[l], gm_out_g[l], w_out[l], ffn_norm_g[l],
                               w_router[l], w_gate[l], w_up[l], w_down[l])
        y_prompt = _layer(y_prompt, *consts)
        y_sample = _layer(y_sample, *consts)
    return (y_prompt, y_sample)
```

```python
import functools

import jax
import jax.numpy as jnp
from jax import lax
from jax.experimental import pallas as pl
from jax.experimental.pallas import tpu as pltpu

F32 = jnp.float32
BF16 = jnp.bfloat16
I32 = jnp.int32

D_MODEL = 2048
HEAD_DIM = 64
ATTN_WIDTH = 1024
KV_WIDTH = 256
GM_WIDTH = 1024
N_Q_HEADS = 16
N_KV_HEADS = 4
GQA_GROUP = 4
GM_GROUPS = 16
CHUNK = 128
N_EXPERTS = 16
EXPERT_FF = 1024
CAPACITY_FACTOR = 2
EPS = 1e-6
NEG_INF = -1e30
LOG2E = 1.4426950408889634
Q_SCALE = HEAD_DIM ** -0.5 * LOG2E

LANES = 128
VMEM_LIMIT = 58 * 1024 * 1024

IN_TOKENS = 512
MIX_TOKENS = 512
FFN_ROWS = 512
FFN_CHUNK = 256
COMB_TOKENS = 512
COMB_PAIRS = 256
COMB_SLOTS = 4
CAST_SLAB_BYTES = 4 * 1024 * 1024

_NT = (((1,), (1,)), ((), ()))


def _const_spec(shape):
    nd = len(shape)
    return pl.BlockSpec(shape, lambda *_: (0,) * nd, pipeline_mode=pl.Buffered(1))


def _split3(a):
    a1 = a.astype(BF16)
    r1 = a - a1.astype(F32)
    a2 = r1.astype(BF16)
    a3 = (r1 - a2.astype(F32)).astype(BF16)
    return a1, a2, a3


def _exact_dot(a, b01):
    a1, a2, a3 = _split3(a)
    d = lambda x: jnp.dot(x, b01, preferred_element_type=F32)
    return (d(a1) + d(a2)) + d(a3)


def _exact_dot_left(a01, b):
    b1, b2, b3 = _split3(b)
    d = lambda x: jnp.dot(a01, x, preferred_element_type=F32)
    return (d(b1) + d(b2)) + d(b3)


def _gelu(x):
    return jax.nn.gelu(x)


def _in_proj_kernel(x_ref, ang_ref, wqkv_ref, wugt_ref, qg_ref, kg_ref, gng_ref, *rest, n_cast):
    cast_in, (q_ref, k_ref, v_ref, ut_ref, gt_ref), cast_out = rest[:n_cast], rest[n_cast:n_cast + 5], rest[n_cast + 5:]
    for src, dst in zip(cast_in, cast_out):
        dst[...] = src[...].astype(BF16)

    xf = x_ref[...]
    ms = jnp.mean(xf * xf, axis=-1, keepdims=True)
    h = (xf * lax.rsqrt(ms + EPS) * ang_ref[...]).astype(BF16)
    tm = xf.shape[0]

    zu = lax.dot_general(wugt_ref[:GM_WIDTH, :], h, _NT, preferred_element_type=F32)
    zg = lax.dot_general(wugt_ref[GM_WIDTH:, :], h, _NT, preferred_element_type=F32)
    ut_ref[...] = _gelu(zu)
    z = jnp.dot(h, wqkv_ref[...], preferred_element_type=F32)
    gl3 = _gelu(zg).reshape(GM_GROUPS, HEAD_DIM, tm)
    msg = jnp.mean(gl3 * gl3, axis=1, keepdims=True)
    gn = gl3 * lax.rsqrt(msg + EPS) * gng_ref[...]
    gt_ref[...] = gn.reshape(GM_WIDTH, tm).astype(BF16)

    low = lax.broadcasted_iota(I32, (1, LANES), 1) < HEAD_DIM

    def two_head_norm(zt, g):
        sq = zt * zt
        s_lo = jnp.sum(jnp.where(low, sq, 0.0), axis=-1, keepdims=True)
        s_hi = jnp.sum(jnp.where(low, 0.0, sq), axis=-1, keepdims=True)
        r = jnp.where(low, lax.rsqrt(s_lo * (1.0 / HEAD_DIM) + EPS), lax.rsqrt(s_hi * (1.0 / HEAD_DIM) + EPS))
        return zt * r * g

    for j in range(ATTN_WIDTH // LANES):
        qn = two_head_norm(z[:, j * LANES:(j + 1) * LANES], qg_ref[...])
        q_ref[:, j * LANES:(j + 1) * LANES] = (qn * Q_SCALE).astype(BF16)
    for j in range(KV_WIDTH // LANES):
        c0 = ATTN_WIDTH + j * LANES
        k_ref[:, j * LANES:(j + 1) * LANES] = two_head_norm(z[:, c0:c0 + LANES], kg_ref[...]).astype(BF16)
    v_ref[...] = z[:, ATTN_WIDTH + KV_WIDTH:].astype(BF16)


def _in_proj(x2d, ang, wqkv, wugt, qg, kg, gng, cast=()):
    t = x2d.shape[0]
    tm = IN_TOKENS
    assert t % tm == 0
    steps = t // tm
    row = lambda i: (i, 0)
    col = lambda i: (0, i)
    cast_specs, cast_shapes = [], []
    for w in cast:
        assert w.shape[0] % steps == 0
        cast_specs.append(pl.BlockSpec((w.shape[0] // steps, w.shape[1]), row))
        cast_shapes.append(jax.ShapeDtypeStruct(w.shape, BF16))
    return pl.pallas_call(
        functools.partial(_in_proj_kernel, n_cast=len(cast)),
        grid=(steps,),
        in_specs=[
            pl.BlockSpec((tm, D_MODEL), row),
            _const_spec((1, D_MODEL)),
            _const_spec((D_MODEL, ATTN_WIDTH + 2 * KV_WIDTH)),
            _const_spec((2 * GM_WIDTH, D_MODEL)),
            _const_spec((1, LANES)),
            _const_spec((1, LANES)),
            _const_spec((GM_GROUPS, HEAD_DIM, 1)),
        ] + cast_specs,
        out_specs=[
            pl.BlockSpec((tm, ATTN_WIDTH), row),
            pl.BlockSpec((tm, KV_WIDTH), row),
            pl.BlockSpec((tm, KV_WIDTH), row),
            pl.BlockSpec((GM_WIDTH, tm), col),
            pl.BlockSpec((GM_WIDTH, tm), col),
        ] + cast_specs,
        out_shape=[
            jax.ShapeDtypeStruct((t, ATTN_WIDTH), BF16),
            jax.ShapeDtypeStruct((t, KV_WIDTH), BF16),
            jax.ShapeDtypeStruct((t, KV_WIDTH), BF16),
            jax.ShapeDtypeStruct((GM_WIDTH, t), F32),
            jax.ShapeDtypeStruct((GM_WIDTH, t), BF16),
        ] + cast_shapes,
        compiler_params=pltpu.CompilerParams(
            dimension_semantics=("arbitrary",), vmem_limit_bytes=VMEM_LIMIT),
        name="in_proj",
    )(x2d, ang, wqkv, wugt, qg, kg, gng, *cast)


def _mixer_kernel(q_ref, kp_ref, kc_ref, kn_ref, vp_ref, vc_ref, vn_ref, ut_ref, gt_ref, x_ref,
                  bias_ref, sink_ref, wsp_ref, bsp_ref, aog_ref, gog_ref, wout_ref, fng_ref,
                  wrs_ref, wrh_ref, x1_ref, aff_ref, *, blocks_per_seq):
    i = pl.program_id(0)
    tb = q_ref.shape[0]
    nsub = tb // CHUNK
    first = (i % blocks_per_seq) == 0
    last = (i % blocks_per_seq) == blocks_per_seq - 1

    kcat = jnp.concatenate([kp_ref[...], kc_ref[...], kn_ref[...]], axis=0)
    vcat = jnp.concatenate([vp_ref[...], vc_ref[...], vn_ref[...]], axis=0)
    key = lax.broadcasted_iota(I32, (1, 3 * CHUNK), 1)
    low = lax.broadcasted_iota(I32, (1, LANES), 1) < HEAD_DIM
    zero = jnp.zeros((), BF16)
    ones_bd = jnp.where((lax.broadcasted_iota(I32, (6 * CHUNK, LANES), 0) < 3 * CHUNK)
                        == (lax.broadcasted_iota(I32, (6 * CHUNK, LANES), 1) < HEAD_DIM), 1.0, 0.0).astype(BF16)

    def swap_halves(t):
        return jnp.concatenate([t[:, HEAD_DIM:], t[:, :HEAD_DIM]], axis=1)

    attn_rows = []
    for sb in range(nsub):
        edge = None
        if sb == 0:
            edge = jnp.where((key < CHUNK) & first, NEG_INF, 0.0)
        if sb == nsub - 1:
            hi = jnp.where((key >= 2 * CHUNK) & last, NEG_INF, 0.0)
            edge = hi if edge is None else edge + hi
        tiles = []
        for jt in range(N_KV_HEADS // 2):
            kt = kcat[sb * CHUNK:sb * CHUNK + 3 * CHUNK, jt * LANES:(jt + 1) * LANES]
            vt = vcat[sb * CHUNK:sb * CHUNK + 3 * CHUNK, jt * LANES:(jt + 1) * LANES]
            kts, vts = swap_halves(kt), swap_halves(vt)
            for jj in range(2):
                j = 2 * jt + jj
                ksame, kswap = (kt, kts) if jj == 0 else (kts, kt)
                vsame, vswap = (vt, vts) if jj == 0 else (vts, vt)
                krhs = jnp.concatenate([jnp.where(low, ksame, zero), jnp.where(low, zero, kswap)], axis=0)
                vrhs = jnp.concatenate([jnp.where(low, vsame, zero), jnp.where(low, zero, vswap)], axis=0)
                vaug = jnp.concatenate([vrhs, ones_bd], axis=1)
                for pr in range(GQA_GROUP // 2):
                    ha = GQA_GROUP * j + 2 * pr
                    qp = q_ref[sb * CHUNK:(sb + 1) * CHUNK, ha * HEAD_DIM:(ha + 2) * HEAD_DIM]
                    s = lax.dot_general(qp, krhs, _NT, preferred_element_type=F32)
                    ps, es = [], []
                    for g in range(2):
                        sh = s[:, g * 3 * CHUNK:(g + 1) * 3 * CHUNK] + bias_ref[ha + g]
                        if edge is not None:
                            sh = sh + edge
                        snk = sink_ref[ha + g]
                        m = jnp.maximum(jnp.max(sh, axis=-1, keepdims=True), snk)
                        ps.append(jnp.exp2(sh - m).astype(BF16))
                        es.append(jnp.exp2(snk - m))
                    o = jnp.dot(jnp.concatenate(ps, axis=1), vaug, preferred_element_type=F32)
                    den = o[:, LANES:] + jnp.where(low, es[0], es[1])
                    tiles.append(o[:, :LANES] / den)
        attn_rows.append(jnp.concatenate(tiles, axis=1))
    attn = jnp.concatenate(attn_rows, axis=0)
    an = attn * lax.rsqrt(jnp.mean(attn * attn, axis=-1, keepdims=True) + EPS) * aog_ref[...]

    st_rows = []
    for pi in range(GM_GROUPS // 2):
        r0 = pi * 2 * HEAD_DIM
        lhs = jnp.concatenate(
            [jnp.concatenate([gt_ref[r0:r0 + HEAD_DIM, c * CHUNK:(c + 1) * CHUNK],
                              gt_ref[r0 + HEAD_DIM:r0 + 2 * HEAD_DIM, c * CHUNK:(c + 1) * CHUNK]], axis=1)
             for c in range(nsub)], axis=0)
        o = jnp.dot(lhs, wsp_ref[pi], preferred_element_type=F32)
        ga = jnp.concatenate([o[c * HEAD_DIM:(c + 1) * HEAD_DIM, :CHUNK] for c in range(nsub)], axis=1)
        gb = jnp.concatenate([o[c * HEAD_DIM:(c + 1) * HEAD_DIM, CHUNK:] for c in range(nsub)], axis=1)
        ba = jnp.concatenate([bsp_ref[2 * pi]] * nsub, axis=1)
        bb = jnp.concatenate([bsp_ref[2 * pi + 1]] * nsub, axis=1)
        st_rows.append(ga + ba)
        st_rows.append(gb + bb)
    gmt = ut_ref[...] * jnp.concatenate(st_rows, axis=0)
    gm = gmt.T
    gn = gm * lax.rsqrt(jnp.mean(gm * gm, axis=-1, keepdims=True) + EPS) * gog_ref[...]

    cat = jnp.concatenate([an.astype(BF16), gn.astype(BF16)], axis=1)
    x1 = x_ref[...] + jnp.dot(cat, wout_ref[...], preferred_element_type=F32)
    x1_ref[...] = x1

    xn = x1 * lax.rsqrt(jnp.mean(x1 * x1, axis=-1, keepdims=True) + EPS) * fng_ref[...]
    xh = xn.astype(BF16)
    xl = (xn - xh.astype(F32)).astype(BF16)
    r1 = lax.dot_general(wrs_ref[...], xh, _NT, preferred_element_type=F32)
    r2 = lax.dot_general(wrh_ref[...], xl, _NT, preferred_element_type=F32)
    logits = (r1[:N_EXPERTS] + r1[N_EXPERTS:]) + r2
    mx = jnp.max(logits, axis=0, keepdims=True)
    ex = jnp.exp(logits - mx)
    aff_ref[...] = ex / jnp.sum(ex, axis=0, keepdims=True)


def _mixer(q, k, v, ut, gt, x2d, seq_len, consts):
    t = q.shape[0]
    tb = MIX_TOKENS
    assert seq_len % tb == 0 and t % seq_len == 0
    r = tb // CHUNK
    nch = t // CHUNK
    row = lambda i: (i, 0)
    col = lambda i: (0, i)
    prev = lambda i: (jnp.maximum(i * r - 1, 0), 0)
    nxt = lambda i: (jnp.minimum((i + 1) * r, nch - 1), 0)
    (bias, sink, wsp, bsp, aog, gog, wout, fng, wrs, wrh) = consts
    kernel = functools.partial(_mixer_kernel, blocks_per_seq=seq_len // tb)
    return pl.pallas_call(
        kernel,
        grid=(t // tb,),
        in_specs=[
            pl.BlockSpec((tb, ATTN_WIDTH), row),
            pl.BlockSpec((CHUNK, KV_WIDTH), prev),
            pl.BlockSpec((tb, KV_WIDTH), row),
            pl.BlockSpec((CHUNK, KV_WIDTH), nxt),
            pl.BlockSpec((CHUNK, KV_WIDTH), prev),
            pl.BlockSpec((tb, KV_WIDTH), row),
            pl.BlockSpec((CHUNK, KV_WIDTH), nxt),
            pl.BlockSpec((GM_WIDTH, tb), col),
            pl.BlockSpec((GM_WIDTH, tb), col),
            pl.BlockSpec((tb, D_MODEL), row),
            _const_spec((N_Q_HEADS, CHUNK, 3 * CHUNK)),
            pl.BlockSpec(memory_space=pltpu.SMEM),
            _const_spec((GM_GROUPS // 2, 2 * CHUNK, 2 * CHUNK)),
            _const_spec((GM_GROUPS, 1, CHUNK)),
            _const_spec((1, ATTN_WIDTH)),
            _const_spec((1, GM_WIDTH)),
            _const_spec((D_MODEL, D_MODEL)),
            _const_spec((1, D_MODEL)),
            _const_spec((2 * N_EXPERTS, D_MODEL)),
            _const_spec((N_EXPERTS, D_MODEL)),
        ],
        out_specs=[
            pl.BlockSpec((tb, D_MODEL), row),
            pl.BlockSpec((N_EXPERTS, tb), col),
        ],
        out_shape=[
            jax.ShapeDtypeStruct((t, D_MODEL), F32),
            jax.ShapeDtypeStruct((N_EXPERTS, t), F32),
        ],
        compiler_params=pltpu.CompilerParams(
            dimension_semantics=("arbitrary",), vmem_limit_bytes=VMEM_LIMIT),
        name="mixer",
    )(q, k, k, k, v, v, v, ut, gt, x2d, bias, sink, wsp, bsp, aog, gog, wout, fng, wrs, wrh)


def _sum12(x):
    return jnp.sum(jnp.sum(x, axis=2, keepdims=True), axis=1, keepdims=True)


def _select_kernel(a_ref, at_ref, m_ref, mt_ref, ppt_ref, ctok_ref, cnt_ref, *, cap, tok_bits):
    ne, nb, _ = a_ref.shape
    a = a_ref[...]
    capf = jnp.float32(cap)

    def thr_step(it, thr):
        cand = thr | jnp.left_shift(jnp.int32(1), 30 - it)
        cnt = _sum12(jnp.where(a >= lax.bitcast_convert_type(cand, F32), 1.0, 0.0))
        return jnp.where(cnt >= capf, cand, thr)

    thr = lax.bitcast_convert_type(lax.fori_loop(0, 31, thr_step, jnp.zeros((ne, 1, 1), I32)), F32)

    gt = a > thr
    eq = a == thr
    need = capf - _sum12(jnp.where(gt, 1.0, 0.0))
    tok = lax.broadcasted_iota(I32, (1, nb, LANES), 1) * LANES + lax.broadcasted_iota(I32, (1, nb, LANES), 2)

    def cut_step(it, c):
        cand = c + jnp.left_shift(jnp.int32(1), tok_bits - 1 - it)
        f = _sum12(jnp.where(eq & (tok < cand), 1.0, 0.0))
        return jnp.where(f < need, cand, c)

    c = lax.fori_loop(0, tok_bits, cut_step, jnp.zeros((ne, 1, 1), I32))
    tcut = jnp.where(need > 0.0, c + 1, 0)

    m = jnp.where(gt | (eq & (tok < tcut)), 1.0, 0.0)
    m_ref[...] = m.astype(BF16)

    a_t = at_ref[...]
    tok_t = lax.broadcasted_iota(I32, (1, LANES, nb), 2) * LANES + lax.broadcasted_iota(I32, (1, LANES, nb), 1)
    mt = jnp.where((a_t > thr) | ((a_t == thr) & (tok_t < tcut)), 1.0, 0.0)
    mt_ref[...] = mt.astype(BF16)

    cnt_t = jnp.sum(mt, axis=0)
    rl = lax.broadcasted_iota(I32, (LANES, LANES), 0)
    cl = lax.broadcasted_iota(I32, (LANES, LANES), 1)
    tri_strict_low = jnp.where(cl < rl, 1.0, 0.0).astype(BF16)
    local_t = jnp.dot(tri_strict_low, cnt_t.astype(BF16), preferred_element_type=F32)
    tot_t = jnp.sum(cnt_t, axis=0, keepdims=True)
    rb = lax.broadcasted_iota(I32, (nb, nb), 0)
    cb = lax.broadcasted_iota(I32, (nb, nb), 1)
    blk_before = jnp.where(rb < cb, 1.0, 0.0).astype(BF16)
    pb_t = _exact_dot(jnp.broadcast_to(tot_t, (8, nb)), blk_before)[0:1]
    ctok_t = local_t + pb_t
    rank = jnp.zeros((LANES, nb), F32)
    for e in range(ne):
        ppt_ref[e] = ctok_t + rank
        rank = rank + mt[e]

    cnt_b = jnp.sum(m, axis=0)
    tri_strict_up = jnp.where(rl < cl, 1.0, 0.0).astype(BF16)
    local_b = jnp.dot(cnt_b.astype(BF16), tri_strict_up, preferred_element_type=F32)
    totb = jnp.dot(cnt_b.astype(BF16), jnp.ones((LANES, LANES), BF16), preferred_element_type=F32)
    blk_after = jnp.where(cb < rb, 1.0, 0.0).astype(BF16)
    ctok_ref[...] = local_b + _exact_dot_left(blk_after, totb)
    cnt_ref[...] = cnt_b


def _select(a_blk, a_t, cap, tok_bits):
    ne, nb, _ = a_blk.shape
    kernel = functools.partial(_select_kernel, cap=cap, tok_bits=tok_bits)
    full3 = lambda s: pl.BlockSpec(s, lambda: (0, 0, 0))
    full2 = lambda s: pl.BlockSpec(s, lambda: (0, 0))
    return pl.pallas_call(
        kernel,
        in_specs=[full3((ne, nb, LANES)), full3((ne, LANES, nb))],
        out_specs=[full3((ne, nb, LANES)), full3((ne, LANES, nb)), full3((ne, LANES, nb)),
                   full2((nb, LANES)), full2((nb, LANES))],
        out_shape=[
            jax.ShapeDtypeStruct((ne, nb, LANES), BF16),
            jax.ShapeDtypeStruct((ne, LANES, nb), BF16),
            jax.ShapeDtypeStruct((ne, LANES, nb), F32),
            jax.ShapeDtypeStruct((nb, LANES), F32),
            jax.ShapeDtypeStruct((nb, LANES), F32),
        ],
        compiler_params=pltpu.CompilerParams(vmem_limit_bytes=VMEM_LIMIT),
        name="select",
    )(a_blk, a_t)


def _compact_kernel(m_ref, mt_ref, at_ref, ppt_ref, idx_ref, gate_ref, pp_ref):
    nb = m_ref.shape[1]
    nchunks = idx_ref.shape[1]
    mb = m_ref[0]
    mt = mt_ref[0]
    a_t = at_ref[0]
    pp_t = ppt_ref[0]

    totb = jnp.dot(mb, jnp.ones((LANES, LANES), BF16), preferred_element_type=F32)
    rb = lax.broadcasted_iota(I32, (nb, nb), 0)
    cb = lax.broadcasted_iota(I32, (nb, nb), 1)
    blk_after = jnp.where(cb < rb, 1.0, 0.0).astype(BF16)
    cb_excl = jnp.dot(blk_after, totb.astype(BF16), preferred_element_type=F32)
    cb_incl = cb_excl + totb
    rl = lax.broadcasted_iota(I32, (LANES, LANES), 0)
    cl = lax.broadcasted_iota(I32, (LANES, LANES), 1)
    tri_low = jnp.where(cl <= rl, 1.0, 0.0).astype(BF16)
    lt = jnp.dot(tri_low, mt, preferred_element_type=F32).astype(BF16)
    sub_b = lax.broadcasted_iota(I32, (nb, LANES), 0).astype(F32)
    sub_l = rl.astype(F32)
    lane = lax.broadcasted_iota(I32, (1, LANES), 1).astype(F32)

    def chunk(c, carry):
        j = lane + lax.convert_element_type(c * LANES, F32)
        bj = jnp.sum(jnp.where(cb_incl <= j, 1.0, 0.0), axis=0, keepdims=True)
        oh = sub_b == bj
        ohb = jnp.where(oh, 1.0, 0.0).astype(BF16)
        jloc = j - jnp.sum(jnp.where(oh, cb_excl, 0.0), axis=0, keepdims=True)
        glt = jnp.dot(lt, ohb, preferred_element_type=F32)
        lj = jnp.sum(jnp.where(glt <= jloc, 1.0, 0.0), axis=0, keepdims=True)
        ohl = sub_l == lj
        gate = jnp.sum(jnp.where(ohl, _exact_dot(a_t, ohb), 0.0), axis=0, keepdims=True)
        pp = jnp.sum(jnp.where(ohl, _exact_dot(pp_t, ohb), 0.0), axis=0, keepdims=True)
        idx_ref[0, pl.ds(c, 1), :] = (bj * LANES + lj).astype(I32)
        gate_ref[0, pl.ds(c, 1), :] = gate
        pp_ref[0, pl.ds(c, 1), :] = pp.astype(I32)
        return carry

    lax.fori_loop(0, nchunks, chunk, 0, unroll=min(4, nchunks))


def _compact(m_blk, m_t, a_t, pp_t, cap):
    ne, nb, _ = m_blk.shape
    assert cap % LANES == 0
    nchunks = cap // LANES
    per_e = lambda s: pl.BlockSpec((1,) + s, lambda e: (e, 0, 0))
    return pl.pallas_call(
        _compact_kernel,
        grid=(ne,),
        in_specs=[per_e((nb, LANES)), per_e((LANES, nb)), per_e((LANES, nb)), per_e((LANES, nb))],
        out_specs=[per_e((nchunks, LANES))] * 3,
        out_shape=[
            jax.ShapeDtypeStruct((ne, nchunks, LANES), I32),
            jax.ShapeDtypeStruct((ne, nchunks, LANES), F32),
            jax.ShapeDtypeStruct((ne, nchunks, LANES), I32),
        ],
        compiler_params=pltpu.CompilerParams(
            dimension_semantics=("arbitrary",), vmem_limit_bytes=VMEM_LIMIT),
        name="compact",
    )(m_blk, m_t, a_t, pp_t)


def _ffn_kernel(idx_ref, idxn_ref, ppp_ref, ppc_ref, gate_ref, x_hbm, fng_ref, wg_ref, wu_ref, wd_ref,
                g_hbm, buf, gsem, ssem, *, tiles_per_expert):
    tm = buf.shape[1]
    step = pl.program_id(0) * tiles_per_expert + pl.program_id(1)
    nsteps = pl.num_programs(0) * tiles_per_expert
    slot = step % 2
    other = 1 - slot

    def gather_copy(row, r, s):
        return pltpu.make_async_copy(x_hbm.at[pl.ds(row, 1), :], buf.at[s, pl.ds(r, 1), :], gsem.at[s])

    def scatter_copy(row, r, s):
        return pltpu.make_async_copy(buf.at[2 + s, pl.ds(r, 1), :], g_hbm.at[pl.ds(row, 1), :], ssem.at[s])

    def wait_gather(s):
        pltpu.make_async_copy(x_hbm.at[pl.ds(0, tm), :], buf.at[s], gsem.at[s]).wait()

    def wait_scatter(s):
        pltpu.make_async_copy(buf.at[2 + s], g_hbm.at[pl.ds(0, tm), :], ssem.at[s]).wait()

    @pl.when(step == 0)
    def _():
        def body(r, carry):
            gather_copy(idx_ref[0, 0, r], r, 0).start()
            return carry
        lax.fori_loop(0, tm, body, 0, unroll=8)
        buf[3] = jnp.zeros((tm, D_MODEL), F32)

    def tile(ps):
        po = 1 - ps

        @pl.when(step > 0)
        def _():
            wait_scatter(ps)

        wait_gather(ps)

        xt = buf[slot]
        yslot = 2 + slot
        xn = (xt * lax.rsqrt(jnp.mean(xt * xt, axis=-1, keepdims=True) + EPS) * fng_ref[...]).astype(BF16)

        ri = lax.broadcasted_iota(I32, (tm, tm), 0)
        ci = lax.broadcasted_iota(I32, (tm, tm), 1)
        eye = jnp.where(ri == ci, 1.0, 0.0).astype(BF16)
        g1, g2, g3 = _split3(jnp.broadcast_to(gate_ref[0], (8, tm)))
        nt = lambda g: lax.dot_general(eye, g, _NT, preferred_element_type=F32)
        gcol = ((nt(g1) + nt(g2)) + nt(g3))[:, 0:1]

        nchunk = EXPERT_FF // FFN_CHUNK
        ngroup = max(nchunk - 1, 1)
        bounds = [tm * k // ngroup for k in range(ngroup + 1)] + [tm] * (nchunk - ngroup)

        def gate_up(c):
            cols = slice(c * FFN_CHUNK, (c + 1) * FFN_CHUNK)
            return (jnp.dot(xn, wg_ref[0, :, cols], preferred_element_type=F32),
                    jnp.dot(xn, wu_ref[0, :, cols], preferred_element_type=F32))

        nxt = gate_up(0)
        for c in range(nchunk):
            for r in range(bounds[c], bounds[c + 1]):
                gather_copy(idxn_ref[0, 0, r], r, po).start()
                scatter_copy(ppp_ref[0, 0, r], r, po).start()
            hg, hu = nxt
            if c + 1 < nchunk:
                nxt = gate_up(c + 1)
            hh = (jax.nn.silu(hg) * hu * gcol).astype(BF16)
            part = jnp.dot(hh, wd_ref[0, c * FFN_CHUNK:(c + 1) * FFN_CHUNK, :], preferred_element_type=F32)
            if c == 0:
                buf[yslot] = part
            else:
                buf[yslot] = buf[yslot] + part

    for ps in range(2):
        pl.when(slot == ps)(functools.partial(tile, ps))

    @pl.when(step + 1 == nsteps)
    def _():
        def body(r, carry):
            scatter_copy(ppc_ref[0, 0, r], r, slot).start()
            return carry
        lax.fori_loop(0, tm, body, 0, unroll=8)
        wait_scatter(other)
        wait_scatter(slot)
        wait_gather(other)


def _ffn(idx_t, gate_t, pp_t, x1, fng, wg, wu, wd, n_pairs):
    ntiles, _, tm = idx_t.shape
    tpe = ntiles // N_EXPERTS
    assert tm % (EXPERT_FF // FFN_CHUNK) == 0
    cur = lambda e, i: (e * tpe + i, 0, 0)
    nxt = lambda e, i: (jnp.minimum(e * tpe + i + 1, ntiles - 1), 0, 0)
    prv = lambda e, i: (e * tpe + i, 0, 0)
    cur1 = lambda e, i: (e * tpe + i + 1, 0, 0)
    wmap = lambda e, i: (e, 0, 0)
    spare = (n_pairs + jnp.arange(tm, dtype=I32)).reshape(1, 1, tm)
    pp_ext = jnp.concatenate([spare, pp_t], axis=0)
    kernel = functools.partial(_ffn_kernel, tiles_per_expert=tpe)
    return pl.pallas_call(
        kernel,
        grid=(N_EXPERTS, tpe),
        in_specs=[
            pl.BlockSpec((1, 1, tm), cur, memory_space=pltpu.SMEM),
            pl.BlockSpec((1, 1, tm), nxt, memory_space=pltpu.SMEM),
            pl.BlockSpec((1, 1, tm), prv, memory_space=pltpu.SMEM),
            pl.BlockSpec((1, 1, tm), cur1, memory_space=pltpu.SMEM),
            pl.BlockSpec((1, 1, tm), cur),
            pl.BlockSpec(memory_space=pl.ANY),
            _const_spec((1, D_MODEL)),
            pl.BlockSpec((1, D_MODEL, EXPERT_FF), wmap),
            pl.BlockSpec((1, D_MODEL, EXPERT_FF), wmap),
            pl.BlockSpec((1, EXPERT_FF, D_MODEL), wmap),
        ],
        out_specs=pl.BlockSpec(memory_space=pl.ANY),
        out_shape=jax.ShapeDtypeStruct((n_pairs + tm, D_MODEL), F32),
        scratch_shapes=[
            pltpu.VMEM((4, tm, D_MODEL), F32),
            pltpu.SemaphoreType.DMA((2,)),
            pltpu.SemaphoreType.DMA((2,)),
        ],
        compiler_params=pltpu.CompilerParams(
            dimension_semantics=("arbitrary", "arbitrary"), vmem_limit_bytes=VMEM_LIMIT),
        name="ffn",
    )(idx_t, idx_t, pp_ext, pp_ext, gate_t, x1, fng, wg, wu, wd)


def _combine_kernel(boff_ref, x1_ref, ctok_ref, cnt_ref, g_hbm, y_ref, gbuf, acc_ref, sem, st_ref, *, n_pairs):
    i = pl.program_id(0)
    tb = x1_ref.shape[0]
    nslot, pc, _ = gbuf.shape
    n_chunks = n_pairs // pc
    r = tb // LANES

    def chunk_copy(g):
        s = g % nslot
        return pltpu.make_async_copy(g_hbm.at[pl.ds(g * pc, pc), :], gbuf.at[s], sem.at[s])

    @pl.when(i == 0)
    def _():
        st_ref[0] = 0
        st_ref[1] = 0

    g_lo = boff_ref[i] // pc
    g_hi = (boff_ref[i + 1] + pc - 1) // pc

    ri = lax.broadcasted_iota(I32, (LANES, LANES), 0)
    ci = lax.broadcasted_iota(I32, (LANES, LANES), 1)
    eye = jnp.where(ri == ci, 1.0, 0.0).astype(BF16)

    def to_col(rowv):
        p1, p2, p3 = _split3(jnp.broadcast_to(rowv, (8, LANES)))
        nt = lambda p: lax.dot_general(eye, p, _NT, preferred_element_type=F32)
        return ((nt(p1) + nt(p2)) + nt(p3))[:, 0:1]

    cex_rows = ctok_ref[pl.ds(i * r, r), :]
    cnt_rows = cnt_ref[pl.ds(i * r, r), :]
    cex = jnp.concatenate([to_col(cex_rows[k:k + 1]) for k in range(r)], axis=0)
    cin = cex + jnp.concatenate([to_col(cnt_rows[k:k + 1]) for k in range(r)], axis=0)

    acc_ref[...] = jnp.zeros_like(acc_ref)

    def body(g, carry):
        def start_one(k, c2):
            chunk_copy(k).start()
            return c2
        upto = jnp.minimum(g + nslot, n_chunks)
        lax.fori_loop(st_ref[0], upto, start_one, 0)
        st_ref[0] = jnp.maximum(st_ref[0], upto)

        @pl.when(g >= st_ref[1])
        def _():
            chunk_copy(g).wait()
            st_ref[1] = g + 1

        p = (g * pc + lax.broadcasted_iota(I32, (1, pc), 1)).astype(F32)
        sel = (cex <= p) & (p < cin)
        acc_ref[...] += jnp.dot(jnp.where(sel, 1.0, 0.0).astype(BF16), gbuf[g % nslot].astype(BF16),
                                preferred_element_type=F32)
        return carry

    lax.fori_loop(g_lo, g_hi, body, 0)
    y_ref[...] = x1_ref[...] + acc_ref[...]


def _combine(boff, x1, ctok, cnt, g, n_pairs):
    t = x1.shape[0]
    tb = COMB_TOKENS
    nb = ctok.shape[0]
    assert t % tb == 0 and n_pairs % COMB_PAIRS == 0
    kernel = functools.partial(_combine_kernel, n_pairs=n_pairs)
    grid_spec = pltpu.PrefetchScalarGridSpec(
        num_scalar_prefetch=1,
        grid=(t // tb,),
        in_specs=[
            pl.BlockSpec((tb, D_MODEL), lambda i, b: (i, 0)),
            pl.BlockSpec((nb, LANES), lambda i, b: (0, 0)),
            pl.BlockSpec((nb, LANES), lambda i, b: (0, 0)),
            pl.BlockSpec(memory_space=pl.ANY),
        ],
        out_specs=pl.BlockSpec((tb, D_MODEL), lambda i, b: (i, 0)),
        scratch_shapes=[
            pltpu.VMEM((COMB_SLOTS, COMB_PAIRS, D_MODEL), F32),
            pltpu.VMEM((tb, D_MODEL), F32),
            pltpu.SemaphoreType.DMA((COMB_SLOTS,)),
            pltpu.SMEM((2,), I32),
        ],
    )
    return pl.pallas_call(
        kernel,
        grid_spec=grid_spec,
        out_shape=jax.ShapeDtypeStruct((t, D_MODEL), F32),
        compiler_params=pltpu.CompilerParams(
            dimension_semantics=("arbitrary",), vmem_limit_bytes=VMEM_LIMIT),
        name="combine",
    )(boff, x1, ctok, cnt, g)


def _alibi_bias():
    slopes = 2.0 ** (-8.0 * jnp.arange(1, N_Q_HEADS + 1, dtype=F32) / N_Q_HEADS)
    qi = jnp.arange(CHUNK)[:, None]
    kj = jnp.arange(3 * CHUNK)[None, :]
    rel = jnp.abs(qi + CHUNK - kj)
    bias = -slopes[:, None, None] * rel.astype(F32)[None] * LOG2E
    return jnp.where((rel <= CHUNK)[None], bias, NEG_INF)


def _prep_weights(attn_norm_g, w_in, q_norm_g, k_norm_g, sink, gm_norm_g, w_spatial, b_spatial,
                  attn_out_g, gm_out_g, w_out, ffn_norm_g, w_router, w_gate, w_up, w_down):
    nqkv = ATTN_WIDTH + 2 * KV_WIDTH
    ang = attn_norm_g.reshape(1, D_MODEL)
    wqkv = w_in[:, :nqkv].astype(BF16)
    wugt = w_in[:, nqkv:].T.astype(BF16)
    qg = jnp.tile(q_norm_g, 2).reshape(1, LANES)
    kg = jnp.tile(k_norm_g, 2).reshape(1, LANES)
    gng = gm_norm_g.reshape(GM_GROUPS, HEAD_DIM, 1)
    wst = jnp.swapaxes(w_spatial, 1, 2).astype(BF16)
    zero = jnp.zeros((GM_GROUPS // 2, CHUNK, CHUNK), BF16)
    wsp = jnp.concatenate([jnp.concatenate([wst[0::2], zero], axis=2),
                           jnp.concatenate([zero, wst[1::2]], axis=2)], axis=1)
    bsp = b_spatial.reshape(GM_GROUPS, 1, CHUNK)
    aog = attn_out_g.reshape(1, ATTN_WIDTH)
    gog = gm_out_g.reshape(1, GM_WIDTH)
    wout = w_out.astype(BF16)
    fng = ffn_norm_g.reshape(1, D_MODEL)
    wr_t = w_router.T
    wr_hi = wr_t.astype(BF16)
    wr_lo = (wr_t - wr_hi.astype(F32)).astype(BF16)
    wrs = jnp.concatenate([wr_hi, wr_lo], axis=0)
    mixer_consts = (_alibi_bias(), sink * LOG2E, wsp, bsp, aog, gog, wout, fng, wrs, wr_hi)
    in_consts = (ang, wqkv, wugt, qg, kg, gng)
    return in_consts, mixer_consts, fng


def _cast_plan(weights, steps):
    views = [w.reshape(-1, w.shape[-1]) for w in weights]
    ok = all(v.shape[0] % steps == 0 and (v.shape[0] // steps) * v.shape[1] * 4 <= CAST_SLAB_BYTES for v in views)
    return views if ok else None


def _front(x, in_consts, mixer_consts, cast_weights):
    b, s, d = x.shape
    t = b * s
    x2d = x.reshape(t, d)
    views = _cast_plan(cast_weights, t // IN_TOKENS)
    outs = _in_proj(x2d, *in_consts, cast=tuple(views) if views is not None else ())
    q, k, v, ut, gt = outs[:5]
    if views is not None:
        cast_done = [o.reshape(w.shape) for o, w in zip(outs[5:], cast_weights)]
    else:
        cast_done = [w.astype(BF16) for w in cast_weights]
    x1, aff_t = _mixer(q, k, v, ut, gt, x2d, s, mixer_consts)

    nb = t // LANES
    cap = max(1, CAPACITY_FACTOR * t // N_EXPERTS)
    tok_bits = max(1, (t - 1).bit_length())
    a_blk = aff_t.reshape(N_EXPERTS, nb, LANES)
    a_t = jnp.swapaxes(a_blk, 1, 2)
    m_blk, m_t, pp_t, ctok, cnt = _select(a_blk, a_t, cap, tok_bits)
    idx, gate, pp = _compact(m_blk, m_t, a_t, pp_t, cap)
    return (x1, idx, gate, pp, ctok, cnt, cap), cast_done


def _back(front, fng, wg, wu, wd, shape):
    x1, idx, gate, pp, ctok, cnt, cap = front
    tm = min(FFN_ROWS, cap)
    ntiles = N_EXPERTS * cap // tm
    n_pairs = N_EXPERTS * cap
    g = _ffn(idx.reshape(ntiles, 1, tm), gate.reshape(ntiles, 1, tm), pp.reshape(ntiles, 1, tm),
             x1, fng, wg, wu, wd, n_pairs)
    boff = jnp.concatenate([ctok.reshape(-1)[::COMB_TOKENS].astype(I32), jnp.full((1,), n_pairs, I32)])
    return _combine(boff, x1, ctok, cnt, g, n_pairs).reshape(shape)


def kernel(x_prompt, x_sample, attn_norm_g, w_in, q_norm_g, k_norm_g, sink, gm_norm_g, w_spatial, b_spatial, attn_out_g, gm_out_g, w_out, ffn_norm_g, w_router, w_gate, w_up, w_down):
    y_prompt, y_sample = x_prompt, x_sample
    for l in range(w_in.shape[0]):
        in_consts, mixer_consts, fng = _prep_weights(
            attn_norm_g[l], w_in[l], q_norm_g[l], k_norm_g[l], sink[l], gm_norm_g[l], w_spatial[l], b_spatial[l],
            attn_out_g[l], gm_out_g[l], w_out[l], ffn_norm_g[l], w_router[l], w_gate[l], w_up[l], w_down[l])
        front_p, (wg,) = _front(y_prompt, in_consts, mixer_consts, [w_gate[l]])
        front_s, (wu, wd) = _front(y_sample, in_consts, mixer_consts, [w_up[l], w_down[l]])
        y_prompt = _back(front_p, fng, wg, wu, wd, y_prompt.shape)
        y_sample = _back(front_s, fng, wg, wu, wd, y_sample.shape)
    return (y_prompt, y_sample)
```

```python
import functools

import jax
import jax.numpy as jnp
from jax import lax
from jax.experimental import pallas as pl
from jax.experimental.pallas import tpu as pltpu

F32 = jnp.float32
BF16 = jnp.bfloat16
I32 = jnp.int32

D_MODEL = 2048
HEAD_DIM = 64
ATTN_WIDTH = 1024
KV_WIDTH = 256
GM_WIDTH = 1024
N_Q_HEADS = 16
N_KV_HEADS = 4
GQA_GROUP = 4
GM_GROUPS = 16
CHUNK = 128
N_EXPERTS = 16
EXPERT_FF = 1024
CAPACITY_FACTOR = 2
EPS = 1e-6
NEG_INF = -1e30
LOG2E = 1.4426950408889634
Q_SCALE = HEAD_DIM ** -0.5 * LOG2E

LANES = 128
VMEM_LIMIT = 58 * 1024 * 1024

IN_TOKENS = 512
MIX_TOKENS = 512
FFN_ROWS = 512
FFN_CHUNK = 256
COMB_TOKENS = 512
COMB_PAIRS = 256
COMB_SLOTS = 4
CAST_SLAB_BYTES = 4 * 1024 * 1024

_NT = (((1,), (1,)), ((), ()))


def _const_spec(shape):
    nd = len(shape)
    return pl.BlockSpec(shape, lambda *_: (0,) * nd, pipeline_mode=pl.Buffered(1))


def _split3(a):
    a1 = a.astype(BF16)
    r1 = a - a1.astype(F32)
    a2 = r1.astype(BF16)
    a3 = (r1 - a2.astype(F32)).astype(BF16)
    return a1, a2, a3


def _exact_dot(a, b01):
    a1, a2, a3 = _split3(a)
    d = lambda x: jnp.dot(x, b01, preferred_element_type=F32)
    return (d(a1) + d(a2)) + d(a3)


def _exact_dot_left(a01, b):
    b1, b2, b3 = _split3(b)
    d = lambda x: jnp.dot(a01, x, preferred_element_type=F32)
    return (d(b1) + d(b2)) + d(b3)


def _gelu(x):
    return jax.nn.gelu(x)


def _in_proj_kernel(x_ref, ang_ref, wqkv_ref, wugt_ref, qg_ref, kg_ref, gng_ref, *rest, n_cast):
    cast_in, (q_ref, k_ref, v_ref, ut_ref, gt_ref), cast_out = rest[:n_cast], rest[n_cast:n_cast + 5], rest[n_cast + 5:]
    for src, dst in zip(cast_in, cast_out):
        dst[...] = src[...].astype(BF16)

    xf = x_ref[...]
    ms = jnp.mean(xf * xf, axis=-1, keepdims=True)
    h = (xf * lax.rsqrt(ms + EPS) * ang_ref[...]).astype(BF16)
    tm = xf.shape[0]

    zu = lax.dot_general(wugt_ref[:GM_WIDTH, :], h, _NT, preferred_element_type=F32)
    zg = lax.dot_general(wugt_ref[GM_WIDTH:, :], h, _NT, preferred_element_type=F32)
    ut_ref[...] = _gelu(zu)
    z = jnp.dot(h, wqkv_ref[...], preferred_element_type=F32)
    gl3 = _gelu(zg).reshape(GM_GROUPS, HEAD_DIM, tm)
    msg = jnp.mean(gl3 * gl3, axis=1, keepdims=True)
    gn = gl3 * lax.rsqrt(msg + EPS) * gng_ref[...]
    gt_ref[...] = gn.reshape(GM_WIDTH, tm).astype(BF16)

    low = lax.broadcasted_iota(I32, (1, LANES), 1) < HEAD_DIM

    def two_head_norm(zt, g):
        sq = zt * zt
        s_lo = jnp.sum(jnp.where(low, sq, 0.0), axis=-1, keepdims=True)
        s_hi = jnp.sum(jnp.where(low, 0.0, sq), axis=-1, keepdims=True)
        r = jnp.where(low, lax.rsqrt(s_lo * (1.0 / HEAD_DIM) + EPS), lax.rsqrt(s_hi * (1.0 / HEAD_DIM) + EPS))
        return zt * r * g

    for j in range(ATTN_WIDTH // LANES):
        qn = two_head_norm(z[:, j * LANES:(j + 1) * LANES], qg_ref[...])
        q_ref[:, j * LANES:(j + 1) * LANES] = (qn * Q_SCALE).astype(BF16)
    for j in range(KV_WIDTH // LANES):
        c0 = ATTN_WIDTH + j * LANES
        k_ref[:, j * LANES:(j + 1) * LANES] = two_head_norm(z[:, c0:c0 + LANES], kg_ref[...]).astype(BF16)
    v_ref[...] = z[:, ATTN_WIDTH + KV_WIDTH:].astype(BF16)


def _in_proj(x2d, ang, wqkv, wugt, qg, kg, gng, cast=()):
    t = x2d.shape[0]
    tm = IN_TOKENS
    assert t % tm == 0
    steps = t // tm
    row = lambda i: (i, 0)
    col = lambda i: (0, i)
    cast_specs, cast_shapes = [], []
    for w in cast:
        assert w.shape[0] % steps == 0
        cast_specs.append(pl.BlockSpec((w.shape[0] // steps, w.shape[1]), row))
        cast_shapes.append(jax.ShapeDtypeStruct(w.shape, BF16))
    return pl.pallas_call(
        functools.partial(_in_proj_kernel, n_cast=len(cast)),
        grid=(steps,),
        in_specs=[
            pl.BlockSpec((tm, D_MODEL), row),
            _const_spec((1, D_MODEL)),
            _const_spec((D_MODEL, ATTN_WIDTH + 2 * KV_WIDTH)),
            _const_spec((2 * GM_WIDTH, D_MODEL)),
            _const_spec((1, LANES)),
            _const_spec((1, LANES)),
            _const_spec((GM_GROUPS, HEAD_DIM, 1)),
        ] + cast_specs,
        out_specs=[
            pl.BlockSpec((tm, ATTN_WIDTH), row),
            pl.BlockSpec((tm, KV_WIDTH), row),
            pl.BlockSpec((tm, KV_WIDTH), row),
            pl.BlockSpec((GM_WIDTH, tm), col),
            pl.BlockSpec((GM_WIDTH, tm), col),
        ] + cast_specs,
        out_shape=[
            jax.ShapeDtypeStruct((t, ATTN_WIDTH), BF16),
            jax.ShapeDtypeStruct((t, KV_WIDTH), BF16),
            jax.ShapeDtypeStruct((t, KV_WIDTH), BF16),
            jax.ShapeDtypeStruct((GM_WIDTH, t), F32),
            jax.ShapeDtypeStruct((GM_WIDTH, t), BF16),
        ] + cast_shapes,
        compiler_params=pltpu.CompilerParams(
            dimension_semantics=("arbitrary",), vmem_limit_bytes=VMEM_LIMIT),
        name="in_proj",
    )(x2d, ang, wqkv, wugt, qg, kg, gng, *cast)


def _mixer_kernel(q_ref, kp_ref, kc_ref, kn_ref, vp_ref, vc_ref, vn_ref, ut_ref, gt_ref, x_ref,
                  bias_ref, sink_ref, wsp_ref, bsp_ref, aog_ref, gog_ref, wout_ref, fng_ref,
                  wrs_ref, wrh_ref, x1_ref, xn_ref, aff_ref, *, blocks_per_seq):
    i = pl.program_id(0)
    tb = q_ref.shape[0]
    nsub = tb // CHUNK
    first = (i % blocks_per_seq) == 0
    last = (i % blocks_per_seq) == blocks_per_seq - 1

    kcat = jnp.concatenate([kp_ref[...], kc_ref[...], kn_ref[...]], axis=0)
    vcat = jnp.concatenate([vp_ref[...], vc_ref[...], vn_ref[...]], axis=0)
    key = lax.broadcasted_iota(I32, (1, 3 * CHUNK), 1)
    low = lax.broadcasted_iota(I32, (1, LANES), 1) < HEAD_DIM
    zero = jnp.zeros((), BF16)
    ones_bd = jnp.where((lax.broadcasted_iota(I32, (6 * CHUNK, LANES), 0) < 3 * CHUNK)
                        == (lax.broadcasted_iota(I32, (6 * CHUNK, LANES), 1) < HEAD_DIM), 1.0, 0.0).astype(BF16)

    def swap_halves(t):
        return jnp.concatenate([t[:, HEAD_DIM:], t[:, :HEAD_DIM]], axis=1)

    attn_rows = []
    for sb in range(nsub):
        edge = None
        if sb == 0:
            edge = jnp.where((key < CHUNK) & first, NEG_INF, 0.0)
        if sb == nsub - 1:
            hi = jnp.where((key >= 2 * CHUNK) & last, NEG_INF, 0.0)
            edge = hi if edge is None else edge + hi
        tiles = []
        for jt in range(N_KV_HEADS // 2):
            kt = kcat[sb * CHUNK:sb * CHUNK + 3 * CHUNK, jt * LANES:(jt + 1) * LANES]
            vt = vcat[sb * CHUNK:sb * CHUNK + 3 * CHUNK, jt * LANES:(jt + 1) * LANES]
            kts, vts = swap_halves(kt), swap_halves(vt)
            for jj in range(2):
                j = 2 * jt + jj
                ksame, kswap = (kt, kts) if jj == 0 else (kts, kt)
                vsame, vswap = (vt, vts) if jj == 0 else (vts, vt)
                krhs = jnp.concatenate([jnp.where(low, ksame, zero), jnp.where(low, zero, kswap)], axis=0)
                vrhs = jnp.concatenate([jnp.where(low, vsame, zero), jnp.where(low, zero, vswap)], axis=0)
                vaug = jnp.concatenate([vrhs, ones_bd], axis=1)
                for pr in range(GQA_GROUP // 2):
                    ha = GQA_GROUP * j + 2 * pr
                    qp = q_ref[sb * CHUNK:(sb + 1) * CHUNK, ha * HEAD_DIM:(ha + 2) * HEAD_DIM]
                    s = lax.dot_general(qp, krhs, _NT, preferred_element_type=F32)
                    ps, es = [], []
                    for g in range(2):
                        sh = s[:, g * 3 * CHUNK:(g + 1) * 3 * CHUNK] + bias_ref[ha + g]
                        if edge is not None:
                            sh = sh + edge
                        snk = sink_ref[ha + g]
                        m = jnp.maximum(jnp.max(sh, axis=-1, keepdims=True), snk)
                        ps.append(jnp.exp2(sh - m).astype(BF16))
                        es.append(jnp.exp2(snk - m))
                    o = jnp.dot(jnp.concatenate(ps, axis=1), vaug, preferred_element_type=F32)
                    den = o[:, LANES:] + jnp.where(low, es[0], es[1])
                    tiles.append(o[:, :LANES] / den)
        attn_rows.append(jnp.concatenate(tiles, axis=1))
    attn = jnp.concatenate(attn_rows, axis=0)
    an = attn * lax.rsqrt(jnp.mean(attn * attn, axis=-1, keepdims=True) + EPS) * aog_ref[...]

    st_rows = []
    for pi in range(GM_GROUPS // 2):
        r0 = pi * 2 * HEAD_DIM
        lhs = jnp.concatenate(
            [jnp.concatenate([gt_ref[r0:r0 + HEAD_DIM, c * CHUNK:(c + 1) * CHUNK],
                              gt_ref[r0 + HEAD_DIM:r0 + 2 * HEAD_DIM, c * CHUNK:(c + 1) * CHUNK]], axis=1)
             for c in range(nsub)], axis=0)
        o = jnp.dot(lhs, wsp_ref[pi], preferred_element_type=F32)
        ga = jnp.concatenate([o[c * HEAD_DIM:(c + 1) * HEAD_DIM, :CHUNK] for c in range(nsub)], axis=1)
        gb = jnp.concatenate([o[c * HEAD_DIM:(c + 1) * HEAD_DIM, CHUNK:] for c in range(nsub)], axis=1)
        ba = jnp.concatenate([bsp_ref[2 * pi]] * nsub, axis=1)
        bb = jnp.concatenate([bsp_ref[2 * pi + 1]] * nsub, axis=1)
        st_rows.append(ga + ba)
        st_rows.append(gb + bb)
    gmt = ut_ref[...] * jnp.concatenate(st_rows, axis=0)
    gm = gmt.T
    gn = gm * lax.rsqrt(jnp.mean(gm * gm, axis=-1, keepdims=True) + EPS) * gog_ref[...]

    cat = jnp.concatenate([an.astype(BF16), gn.astype(BF16)], axis=1)
    x1 = x_ref[...] + jnp.dot(cat, wout_ref[...], preferred_element_type=F32)
    x1_ref[...] = x1

    xn = x1 * lax.rsqrt(jnp.mean(x1 * x1, axis=-1, keepdims=True) + EPS) * fng_ref[...]
    xn_ref[...] = xn
    xh = xn.astype(BF16)
    xl = (xn - xh.astype(F32)).astype(BF16)
    r1 = lax.dot_general(wrs_ref[...], xh, _NT, preferred_element_type=F32)
    r2 = lax.dot_general(wrh_ref[...], xl, _NT, preferred_element_type=F32)
    logits = (r1[:N_EXPERTS] + r1[N_EXPERTS:]) + r2
    mx = jnp.max(logits, axis=0, keepdims=True)
    ex = jnp.exp(logits - mx)
    aff_ref[...] = ex / jnp.sum(ex, axis=0, keepdims=True)


def _mixer(q, k, v, ut, gt, x2d, seq_len, consts):
    t = q.shape[0]
    tb = MIX_TOKENS
    assert seq_len % tb == 0 and t % seq_len == 0
    r = tb // CHUNK
    nch = t // CHUNK
    row = lambda i: (i, 0)
    col = lambda i: (0, i)
    prev = lambda i: (jnp.maximum(i * r - 1, 0), 0)
    nxt = lambda i: (jnp.minimum((i + 1) * r, nch - 1), 0)
    (bias, sink, wsp, bsp, aog, gog, wout, fng, wrs, wrh) = consts
    kernel = functools.partial(_mixer_kernel, blocks_per_seq=seq_len // tb)
    return pl.pallas_call(
        kernel,
        grid=(t // tb,),
        in_specs=[
            pl.BlockSpec((tb, ATTN_WIDTH), row),
            pl.BlockSpec((CHUNK, KV_WIDTH), prev),
            pl.BlockSpec((tb, KV_WIDTH), row),
            pl.BlockSpec((CHUNK, KV_WIDTH), nxt),
            pl.BlockSpec((CHUNK, KV_WIDTH), prev),
            pl.BlockSpec((tb, KV_WIDTH), row),
            pl.BlockSpec((CHUNK, KV_WIDTH), nxt),
            pl.BlockSpec((GM_WIDTH, tb), col),
            pl.BlockSpec((GM_WIDTH, tb), col),
            pl.BlockSpec((tb, D_MODEL), row),
            _const_spec((N_Q_HEADS, CHUNK, 3 * CHUNK)),
            pl.BlockSpec(memory_space=pltpu.SMEM),
            _const_spec((GM_GROUPS // 2, 2 * CHUNK, 2 * CHUNK)),
            _const_spec((GM_GROUPS, 1, CHUNK)),
            _const_spec((1, ATTN_WIDTH)),
            _const_spec((1, GM_WIDTH)),
            _const_spec((D_MODEL, D_MODEL)),
            _const_spec((1, D_MODEL)),
            _const_spec((2 * N_EXPERTS, D_MODEL)),
            _const_spec((N_EXPERTS, D_MODEL)),
        ],
        out_specs=[
            pl.BlockSpec((tb, D_MODEL), row),
            pl.BlockSpec((tb, D_MODEL), row),
            pl.BlockSpec((N_EXPERTS, tb), col),
        ],
        out_shape=[
            jax.ShapeDtypeStruct((t, D_MODEL), F32),
            jax.ShapeDtypeStruct((t, D_MODEL), F32),
            jax.ShapeDtypeStruct((N_EXPERTS, t), F32),
        ],
        compiler_params=pltpu.CompilerParams(
            dimension_semantics=("arbitrary",), vmem_limit_bytes=VMEM_LIMIT),
        name="mixer",
    )(q, k, k, k, v, v, v, ut, gt, x2d, bias, sink, wsp, bsp, aog, gog, wout, fng, wrs, wrh)


def _sum12(x):
    return jnp.sum(jnp.sum(x, axis=2, keepdims=True), axis=1, keepdims=True)


def _select_kernel(a_ref, m_ref, mt_ref, at_ref, ppt_ref, ctok_ref, cnt_ref, *, cap, tok_bits):
    ne, nb, _ = a_ref.shape
    a = a_ref[...]
    capf = jnp.float32(cap)

    def thr_step(it, thr):
        cand = thr | jnp.left_shift(jnp.int32(1), 30 - it)
        cnt = _sum12(jnp.where(a >= lax.bitcast_convert_type(cand, F32), 1.0, 0.0))
        return jnp.where(cnt >= capf, cand, thr)

    thr = lax.bitcast_convert_type(lax.fori_loop(0, 31, thr_step, jnp.zeros((ne, 1, 1), I32)), F32)

    gt = a > thr
    eq = a == thr
    need = capf - _sum12(jnp.where(gt, 1.0, 0.0))
    tok = lax.broadcasted_iota(I32, (1, nb, LANES), 1) * LANES + lax.broadcasted_iota(I32, (1, nb, LANES), 2)

    def cut_step(it, c):
        cand = c + jnp.left_shift(jnp.int32(1), tok_bits - 1 - it)
        f = _sum12(jnp.where(eq & (tok < cand), 1.0, 0.0))
        return jnp.where(f < need, cand, c)

    c = lax.fori_loop(0, tok_bits, cut_step, jnp.zeros((ne, 1, 1), I32))
    tcut = jnp.where(need > 0.0, c + 1, 0)

    m = jnp.where(gt | (eq & (tok < tcut)), 1.0, 0.0)
    m_ref[...] = m.astype(BF16)

    tok_t = lax.broadcasted_iota(I32, (LANES, nb), 1) * LANES + lax.broadcasted_iota(I32, (LANES, nb), 0)
    mt = []
    for e in range(ne):
        a_te = a_ref[e].T
        at_ref[e] = a_te
        mt_e = jnp.where((a_te > thr[e]) | ((a_te == thr[e]) & (tok_t < tcut[e])), 1.0, 0.0)
        mt_ref[e] = mt_e.astype(BF16)
        mt.append(mt_e)

    cnt_t = functools.reduce(lambda x, y: x + y, mt)
    rl = lax.broadcasted_iota(I32, (LANES, LANES), 0)
    cl = lax.broadcasted_iota(I32, (LANES, LANES), 1)
    tri_strict_low = jnp.where(cl < rl, 1.0, 0.0).astype(BF16)
    local_t = jnp.dot(tri_strict_low, cnt_t.astype(BF16), preferred_element_type=F32)
    tot_t = jnp.sum(cnt_t, axis=0, keepdims=True)
    rb = lax.broadcasted_iota(I32, (nb, nb), 0)
    cb = lax.broadcasted_iota(I32, (nb, nb), 1)
    blk_before = jnp.where(rb < cb, 1.0, 0.0).astype(BF16)
    pb_t = _exact_dot(jnp.broadcast_to(tot_t, (8, nb)), blk_before)[0:1]
    ctok_t = local_t + pb_t
    rank = jnp.zeros((LANES, nb), F32)
    for e in range(ne):
        ppt_ref[e] = ctok_t + rank
        rank = rank + mt[e]

    cnt_b = jnp.sum(m, axis=0)
    tri_strict_up = jnp.where(rl < cl, 1.0, 0.0).astype(BF16)
    local_b = jnp.dot(cnt_b.astype(BF16), tri_strict_up, preferred_element_type=F32)
    totb = jnp.dot(cnt_b.astype(BF16), jnp.ones((LANES, LANES), BF16), preferred_element_type=F32)
    blk_after = jnp.where(cb < rb, 1.0, 0.0).astype(BF16)
    ctok_ref[...] = local_b + _exact_dot_left(blk_after, totb)
    cnt_ref[...] = cnt_b


def _select(a_blk, cap, tok_bits):
    ne, nb, _ = a_blk.shape
    kernel = functools.partial(_select_kernel, cap=cap, tok_bits=tok_bits)
    full3 = lambda s: pl.BlockSpec(s, lambda: (0, 0, 0))
    full2 = lambda s: pl.BlockSpec(s, lambda: (0, 0))
    return pl.pallas_call(
        kernel,
        in_specs=[full3((ne, nb, LANES))],
        out_specs=[full3((ne, nb, LANES)), full3((ne, LANES, nb)), full3((ne, LANES, nb)), full3((ne, LANES, nb)),
                   full2((nb, LANES)), full2((nb, LANES))],
        out_shape=[
            jax.ShapeDtypeStruct((ne, nb, LANES), BF16),
            jax.ShapeDtypeStruct((ne, LANES, nb), BF16),
            jax.ShapeDtypeStruct((ne, LANES, nb), F32),
            jax.ShapeDtypeStruct((ne, LANES, nb), F32),
            jax.ShapeDtypeStruct((nb, LANES), F32),
            jax.ShapeDtypeStruct((nb, LANES), F32),
        ],
        compiler_params=pltpu.CompilerParams(vmem_limit_bytes=VMEM_LIMIT),
        name="select",
    )(a_blk)


def _compact_kernel(m_ref, mt_ref, at_ref, ppt_ref, idx_ref, gate_ref, pp_ref):
    nb = m_ref.shape[1]
    nchunks = idx_ref.shape[1]
    mb = m_ref[0]
    mt = mt_ref[0]
    a_t = at_ref[0]
    pp_t = ppt_ref[0]

    totb = jnp.dot(mb, jnp.ones((LANES, LANES), BF16), preferred_element_type=F32)
    rb = lax.broadcasted_iota(I32, (nb, nb), 0)
    cb = lax.broadcasted_iota(I32, (nb, nb), 1)
    blk_after = jnp.where(cb < rb, 1.0, 0.0).astype(BF16)
    cb_excl = jnp.dot(blk_after, totb.astype(BF16), preferred_element_type=F32)
    cb_incl = cb_excl + totb
    rl = lax.broadcasted_iota(I32, (LANES, LANES), 0)
    cl = lax.broadcasted_iota(I32, (LANES, LANES), 1)
    tri_low = jnp.where(cl <= rl, 1.0, 0.0).astype(BF16)
    lt = jnp.dot(tri_low, mt, preferred_element_type=F32).astype(BF16)
    sub_b = lax.broadcasted_iota(I32, (nb, LANES), 0).astype(F32)
    sub_l = rl.astype(F32)
    lane = lax.broadcasted_iota(I32, (1, LANES), 1).astype(F32)

    def chunk(c, carry):
        j = lane + lax.convert_element_type(c * LANES, F32)
        bj = jnp.sum(jnp.where(cb_incl <= j, 1.0, 0.0), axis=0, keepdims=True)
        oh = sub_b == bj
        ohb = jnp.where(oh, 1.0, 0.0).astype(BF16)
        jloc = j - jnp.sum(jnp.where(oh, cb_excl, 0.0), axis=0, keepdims=True)
        glt = jnp.dot(lt, ohb, preferred_element_type=F32)
        lj = jnp.sum(jnp.where(glt <= jloc, 1.0, 0.0), axis=0, keepdims=True)
        ohl = sub_l == lj
        gate = jnp.sum(jnp.where(ohl, _exact_dot(a_t, ohb), 0.0), axis=0, keepdims=True)
        pp = jnp.sum(jnp.where(ohl, _exact_dot(pp_t, ohb), 0.0), axis=0, keepdims=True)
        idx_ref[0, pl.ds(c, 1), :] = (bj * LANES + lj).astype(I32)
        gate_ref[0, pl.ds(c, 1), :] = gate
        pp_ref[0, pl.ds(c, 1), :] = pp.astype(I32)
        return carry

    lax.fori_loop(0, nchunks, chunk, 0, unroll=min(4, nchunks))


def _compact(m_blk, m_t, a_t, pp_t, cap):
    ne, nb, _ = m_blk.shape
    assert cap % LANES == 0
    nchunks = cap // LANES
    per_e = lambda s: pl.BlockSpec((1,) + s, lambda e: (e, 0, 0))
    return pl.pallas_call(
        _compact_kernel,
        grid=(ne,),
        in_specs=[per_e((nb, LANES)), per_e((LANES, nb)), per_e((LANES, nb)), per_e((LANES, nb))],
        out_specs=[per_e((nchunks, LANES))] * 3,
        out_shape=[
            jax.ShapeDtypeStruct((ne, nchunks, LANES), I32),
            jax.ShapeDtypeStruct((ne, nchunks, LANES), F32),
            jax.ShapeDtypeStruct((ne, nchunks, LANES), I32),
        ],
        compiler_params=pltpu.CompilerParams(
            dimension_semantics=("arbitrary",), vmem_limit_bytes=VMEM_LIMIT),
        name="compact",
    )(m_blk, m_t, a_t, pp_t)


def _ffn_kernel(idx_ref, idxn_ref, ppp_ref, ppc_ref, gate_ref, x_hbm, wg_ref, wu_ref, wd_ref,
                g_hbm, buf, gsem, ssem, *, tiles_per_expert):
    tm = buf.shape[1]
    step = pl.program_id(0) * tiles_per_expert + pl.program_id(1)
    nsteps = pl.num_programs(0) * tiles_per_expert
    slot = step % 2
    other = 1 - slot

    def gather_copy(row, r, s):
        return pltpu.make_async_copy(x_hbm.at[pl.ds(row, 1), :], buf.at[s, pl.ds(r, 1), :], gsem.at[s])

    def scatter_copy(row, r, s):
        return pltpu.make_async_copy(buf.at[2 + s, pl.ds(r, 1), :], g_hbm.at[pl.ds(row, 1), :], ssem.at[s])

    def wait_gather(s):
        pltpu.make_async_copy(x_hbm.at[pl.ds(0, tm), :], buf.at[s], gsem.at[s]).wait()

    def wait_scatter(s):
        pltpu.make_async_copy(buf.at[2 + s], g_hbm.at[pl.ds(0, tm), :], ssem.at[s]).wait()

    @pl.when(step == 0)
    def _():
        def body(r, carry):
            gather_copy(idx_ref[0, 0, r], r, 0).start()
            return carry
        lax.fori_loop(0, tm, body, 0, unroll=8)
        buf[3] = jnp.zeros((tm, D_MODEL), F32)

    def tile(ps):
        po = 1 - ps

        @pl.when(step > 0)
        def _():
            wait_scatter(ps)

        wait_gather(ps)

        xn = buf[slot].astype(BF16)
        yslot = 2 + slot

        ri = lax.broadcasted_iota(I32, (tm, tm), 0)
        ci = lax.broadcasted_iota(I32, (tm, tm), 1)
        eye = jnp.where(ri == ci, 1.0, 0.0).astype(BF16)
        g1, g2, g3 = _split3(jnp.broadcast_to(gate_ref[0], (8, tm)))
        nt = lambda g: lax.dot_general(eye, g, _NT, preferred_element_type=F32)
        gcol = ((nt(g1) + nt(g2)) + nt(g3))[:, 0:1]

        nchunk = EXPERT_FF // FFN_CHUNK
        ngroup = max(nchunk - 1, 1)
        bounds = [tm * k // ngroup for k in range(ngroup + 1)] + [tm] * (nchunk - ngroup)

        def gate_up(c):
            cols = slice(c * FFN_CHUNK, (c + 1) * FFN_CHUNK)
            return (jnp.dot(xn, wg_ref[0, :, cols], preferred_element_type=F32),
                    jnp.dot(xn, wu_ref[0, :, cols], preferred_element_type=F32))

        nxt = gate_up(0)
        for c in range(nchunk):
            for r in range(bounds[c], bounds[c + 1]):
                gather_copy(idxn_ref[0, 0, r], r, po).start()
                scatter_copy(ppp_ref[0, 0, r], r, po).start()
            hg, hu = nxt
            if c + 1 < nchunk:
                nxt = gate_up(c + 1)
            hh = (jax.nn.silu(hg) * hu * gcol).astype(BF16)
            part = jnp.dot(hh, wd_ref[0, c * FFN_CHUNK:(c + 1) * FFN_CHUNK, :], preferred_element_type=F32)
            if c == 0:
                buf[yslot] = part
            else:
                buf[yslot] = buf[yslot] + part

    for ps in range(2):
        pl.when(slot == ps)(functools.partial(tile, ps))

    @pl.when(step + 1 == nsteps)
    def _():
        def body(r, carry):
            scatter_copy(ppc_ref[0, 0, r], r, slot).start()
            return carry
        lax.fori_loop(0, tm, body, 0, unroll=8)
        wait_scatter(other)
        wait_scatter(slot)
        wait_gather(other)


def _ffn(idx_t, gate_t, pp_t, xn, wg, wu, wd, n_pairs):
    ntiles, _, tm = idx_t.shape
    tpe = ntiles // N_EXPERTS
    assert tm % (EXPERT_FF // FFN_CHUNK) == 0
    cur = lambda e, i: (e * tpe + i, 0, 0)
    nxt = lambda e, i: (jnp.minimum(e * tpe + i + 1, ntiles - 1), 0, 0)
    prv = lambda e, i: (e * tpe + i, 0, 0)
    cur1 = lambda e, i: (e * tpe + i + 1, 0, 0)
    wmap = lambda e, i: (e, 0, 0)
    spare = (n_pairs + jnp.arange(tm, dtype=I32)).reshape(1, 1, tm)
    pp_ext = jnp.concatenate([spare, pp_t], axis=0)
    kernel = functools.partial(_ffn_kernel, tiles_per_expert=tpe)
    return pl.pallas_call(
        kernel,
        grid=(N_EXPERTS, tpe),
        in_specs=[
            pl.BlockSpec((1, 1, tm), cur, memory_space=pltpu.SMEM),
            pl.BlockSpec((1, 1, tm), nxt, memory_space=pltpu.SMEM),
            pl.BlockSpec((1, 1, tm), prv, memory_space=pltpu.SMEM),
            pl.BlockSpec((1, 1, tm), cur1, memory_space=pltpu.SMEM),
            pl.BlockSpec((1, 1, tm), cur),
            pl.BlockSpec(memory_space=pl.ANY),
            pl.BlockSpec((1, D_MODEL, EXPERT_FF), wmap),
            pl.BlockSpec((1, D_MODEL, EXPERT_FF), wmap),
            pl.BlockSpec((1, EXPERT_FF, D_MODEL), wmap),
        ],
        out_specs=pl.BlockSpec(memory_space=pl.ANY),
        out_shape=jax.ShapeDtypeStruct((n_pairs + tm, D_MODEL), F32),
        scratch_shapes=[
            pltpu.VMEM((4, tm, D_MODEL), F32),
            pltpu.SemaphoreType.DMA((2,)),
            pltpu.SemaphoreType.DMA((2,)),
        ],
        compiler_params=pltpu.CompilerParams(
            dimension_semantics=("arbitrary", "arbitrary"), vmem_limit_bytes=VMEM_LIMIT),
        name="ffn",
    )(idx_t, idx_t, pp_ext, pp_ext, gate_t, xn, wg, wu, wd)


def _combine_kernel(boff_ref, x1_ref, ctok_ref, cnt_ref, g_hbm, y_ref, gbuf, acc_ref, sem, st_ref, *, n_pairs):
    i = pl.program_id(0)
    tb = x1_ref.shape[0]
    nslot, pc, _ = gbuf.shape
    n_chunks = n_pairs // pc
    r = tb // LANES

    def chunk_copy(g):
        s = g % nslot
        return pltpu.make_async_copy(g_hbm.at[pl.ds(g * pc, pc), :], gbuf.at[s], sem.at[s])

    @pl.when(i == 0)
    def _():
        st_ref[0] = 0
        st_ref[1] = 0

    g_lo = boff_ref[i] // pc
    g_hi = (boff_ref[i + 1] + pc - 1) // pc

    ri = lax.broadcasted_iota(I32, (LANES, LANES), 0)
    ci = lax.broadcasted_iota(I32, (LANES, LANES), 1)
    eye = jnp.where(ri == ci, 1.0, 0.0).astype(BF16)

    def to_col(rowv):
        p1, p2, p3 = _split3(jnp.broadcast_to(rowv, (8, LANES)))
        nt = lambda p: lax.dot_general(eye, p, _NT, preferred_element_type=F32)
        return ((nt(p1) + nt(p2)) + nt(p3))[:, 0:1]

    cex_rows = ctok_ref[pl.ds(i * r, r), :]
    cnt_rows = cnt_ref[pl.ds(i * r, r), :]
    cex = jnp.concatenate([to_col(cex_rows[k:k + 1]) for k in range(r)], axis=0)
    cin = cex + jnp.concatenate([to_col(cnt_rows[k:k + 1]) for k in range(r)], axis=0)

    acc_ref[...] = jnp.zeros_like(acc_ref)

    def body(g, carry):
        def start_one(k, c2):
            chunk_copy(k).start()
            return c2
        upto = jnp.minimum(g + nslot, n_chunks)
        lax.fori_loop(st_ref[0], upto, start_one, 0)
        st_ref[0] = jnp.maximum(st_ref[0], upto)

        @pl.when(g >= st_ref[1])
        def _():
            chunk_copy(g).wait()
            st_ref[1] = g + 1

        p = (g * pc + lax.broadcasted_iota(I32, (1, pc), 1)).astype(F32)
        sel = (cex <= p) & (p < cin)
        acc_ref[...] += jnp.dot(jnp.where(sel, 1.0, 0.0).astype(BF16), gbuf[g % nslot].astype(BF16),
                                preferred_element_type=F32)
        return carry

    lax.fori_loop(g_lo, g_hi, body, 0)
    y_ref[...] = x1_ref[...] + acc_ref[...]


def _combine(boff, x1, ctok, cnt, g, n_pairs):
    t = x1.shape[0]
    tb = COMB_TOKENS
    nb = ctok.shape[0]
    assert t % tb == 0 and n_pairs % COMB_PAIRS == 0
    kernel = functools.partial(_combine_kernel, n_pairs=n_pairs)
    grid_spec = pltpu.PrefetchScalarGridSpec(
        num_scalar_prefetch=1,
        grid=(t // tb,),
        in_specs=[
            pl.BlockSpec((tb, D_MODEL), lambda i, b: (i, 0)),
            pl.BlockSpec((nb, LANES), lambda i, b: (0, 0)),
            pl.BlockSpec((nb, LANES), lambda i, b: (0, 0)),
            pl.BlockSpec(memory_space=pl.ANY),
        ],
        out_specs=pl.BlockSpec((tb, D_MODEL), lambda i, b: (i, 0)),
        scratch_shapes=[
            pltpu.VMEM((COMB_SLOTS, COMB_PAIRS, D_MODEL), F32),
            pltpu.VMEM((tb, D_MODEL), F32),
            pltpu.SemaphoreType.DMA((COMB_SLOTS,)),
            pltpu.SMEM((2,), I32),
        ],
    )
    return pl.pallas_call(
        kernel,
        grid_spec=grid_spec,
        out_shape=jax.ShapeDtypeStruct((t, D_MODEL), F32),
        compiler_params=pltpu.CompilerParams(
            dimension_semantics=("arbitrary",), vmem_limit_bytes=VMEM_LIMIT),
        name="combine",
    )(boff, x1, ctok, cnt, g)


def _alibi_bias():
    slopes = 2.0 ** (-8.0 * jnp.arange(1, N_Q_HEADS + 1, dtype=F32) / N_Q_HEADS)
    qi = jnp.arange(CHUNK)[:, None]
    kj = jnp.arange(3 * CHUNK)[None, :]
    rel = jnp.abs(qi + CHUNK - kj)
    bias = -slopes[:, None, None] * rel.astype(F32)[None] * LOG2E
    return jnp.where((rel <= CHUNK)[None], bias, NEG_INF)


def _prep_weights(attn_norm_g, w_in, q_norm_g, k_norm_g, sink, gm_norm_g, w_spatial, b_spatial,
                  attn_out_g, gm_out_g, w_out, ffn_norm_g, w_router, w_gate, w_up, w_down):
    nqkv = ATTN_WIDTH + 2 * KV_WIDTH
    ang = attn_norm_g.reshape(1, D_MODEL)
    wqkv = w_in[:, :nqkv].astype(BF16)
    wugt = w_in[:, nqkv:].T.astype(BF16)
    qg = jnp.tile(q_norm_g, 2).reshape(1, LANES)
    kg = jnp.tile(k_norm_g, 2).reshape(1, LANES)
    gng = gm_norm_g.reshape(GM_GROUPS, HEAD_DIM, 1)
    wst = jnp.swapaxes(w_spatial, 1, 2).astype(BF16)
    zero = jnp.zeros((GM_GROUPS // 2, CHUNK, CHUNK), BF16)
    wsp = jnp.concatenate([jnp.concatenate([wst[0::2], zero], axis=2),
                           jnp.concatenate([zero, wst[1::2]], axis=2)], axis=1)
    bsp = b_spatial.reshape(GM_GROUPS, 1, CHUNK)
    aog = attn_out_g.reshape(1, ATTN_WIDTH)
    gog = gm_out_g.reshape(1, GM_WIDTH)
    wout = w_out.astype(BF16)
    fng = ffn_norm_g.reshape(1, D_MODEL)
    wr_t = w_router.T
    wr_hi = wr_t.astype(BF16)
    wr_lo = (wr_t - wr_hi.astype(F32)).astype(BF16)
    wrs = jnp.concatenate([wr_hi, wr_lo], axis=0)
    mixer_consts = (_alibi_bias(), sink * LOG2E, wsp, bsp, aog, gog, wout, fng, wrs, wr_hi)
    in_consts = (ang, wqkv, wugt, qg, kg, gng)
    return in_consts, mixer_consts


def _cast_plan(weights, steps):
    views = [w.reshape(-1, w.shape[-1]) for w in weights]
    ok = all(v.shape[0] % steps == 0 and (v.shape[0] // steps) * v.shape[1] * 4 <= CAST_SLAB_BYTES for v in views)
    return views if ok else None


def _front(x, in_consts, mixer_consts, cast_weights):
    b, s, d = x.shape
    t = b * s
    x2d = x.reshape(t, d)
    views = _cast_plan(cast_weights, t // IN_TOKENS)
    outs = _in_proj(x2d, *in_consts, cast=tuple(views) if views is not None else ())
    q, k, v, ut, gt = outs[:5]
    if views is not None:
        cast_done = [o.reshape(w.shape) for o, w in zip(outs[5:], cast_weights)]
    else:
        cast_done = [w.astype(BF16) for w in cast_weights]
    x1, xn, aff_t = _mixer(q, k, v, ut, gt, x2d, s, mixer_consts)

    nb = t // LANES
    cap = max(1, CAPACITY_FACTOR * t // N_EXPERTS)
    tok_bits = max(1, (t - 1).bit_length())
    a_blk = aff_t.reshape(N_EXPERTS, nb, LANES)
    m_blk, m_t, a_t, pp_t, ctok, cnt = _select(a_blk, cap, tok_bits)
    idx, gate, pp = _compact(m_blk, m_t, a_t, pp_t, cap)
    return (x1, xn, idx, gate, pp, ctok, cnt, cap), cast_done


def _back(front, wg, wu, wd, shape):
    x1, xn, idx, gate, pp, ctok, cnt, cap = front
    tm = min(FFN_ROWS, cap)
    ntiles = N_EXPERTS * cap // tm
    n_pairs = N_EXPERTS * cap
    g = _ffn(idx.reshape(ntiles, 1, tm), gate.reshape(ntiles, 1, tm), pp.reshape(ntiles, 1, tm),
             xn, wg, wu, wd, n_pairs)
    boff = jnp.concatenate([ctok.reshape(-1)[::COMB_TOKENS].astype(I32), jnp.full((1,), n_pairs, I32)])
    return _combine(boff, x1, ctok, cnt, g, n_pairs).reshape(shape)


def kernel(x_prompt, x_sample, attn_norm_g, w_in, q_norm_g, k_norm_g, sink, gm_norm_g, w_spatial, b_spatial, attn_out_g, gm_out_g, w_out, ffn_norm_g, w_router, w_gate, w_up, w_down):
    y_prompt, y_sample = x_prompt, x_sample
    for l in range(w_in.shape[0]):
        in_consts, mixer_consts = _prep_weights(
            attn_norm_g[l], w_in[l], q_norm_g[l], k_norm_g[l], sink[l], gm_norm_g[l], w_spatial[l], b_spatial[l],
            attn_out_g[l], gm_out_g[l], w_out[l], ffn_norm_g[l], w_router[l], w_gate[l], w_up[l], w_down[l])
        front_p, (wg,) = _front(y_prompt, in_consts, mixer_consts, [w_gate[l]])
        front_s, (wu, wd) = _front(y_sample, in_consts, mixer_consts, [w_up[l], w_down[l]])
        y_prompt = _back(front_p, wg, wu, wd, y_prompt.shape)
        y_sample = _back(front_s, wg, wu, wd, y_sample.shape)
    return (y_prompt, y_sample)
```

```python
import functools

import jax
import jax.numpy as jnp
from jax import lax
from jax.experimental import pallas as pl
from jax.experimental.pallas import tpu as pltpu

F32 = jnp.float32
BF16 = jnp.bfloat16
I32 = jnp.int32

D_MODEL = 2048
HEAD_DIM = 64
ATTN_WIDTH = 1024
KV_WIDTH = 256
GM_WIDTH = 1024
N_Q_HEADS = 16
N_KV_HEADS = 4
GQA_GROUP = 4
GM_GROUPS = 16
CHUNK = 128
N_EXPERTS = 16
EXPERT_FF = 1024
CAPACITY_FACTOR = 2
EPS = 1e-6
NEG_INF = -1e30
LOG2E = 1.4426950408889634
Q_SCALE = HEAD_DIM ** -0.5 * LOG2E

LANES = 128
VMEM_LIMIT = 58 * 1024 * 1024

IN_TOKENS = 512
MIX_TOKENS = 512
FFN_ROWS = 512
FFN_CHUNK = 256
FFN_ROW_BLOCKS = 2
COMB_TOKENS = 512
COMB_PAIRS = 256
COMB_SLOTS = 4
CAST_SLAB_BYTES = 4 * 1024 * 1024

_NT = (((1,), (1,)), ((), ()))


def _const_spec(shape):
    nd = len(shape)
    return pl.BlockSpec(shape, lambda *_: (0,) * nd, pipeline_mode=pl.Buffered(1))


def _split3(a):
    a1 = a.astype(BF16)
    r1 = a - a1.astype(F32)
    a2 = r1.astype(BF16)
    a3 = (r1 - a2.astype(F32)).astype(BF16)
    return a1, a2, a3


def _exact_dot(a, b01):
    a1, a2, a3 = _split3(a)
    d = lambda x: jnp.dot(x, b01, preferred_element_type=F32)
    return (d(a1) + d(a2)) + d(a3)


def _exact_dot_left(a01, b):
    b1, b2, b3 = _split3(b)
    d = lambda x: jnp.dot(a01, x, preferred_element_type=F32)
    return (d(b1) + d(b2)) + d(b3)


def _gelu(x):
    return jax.nn.gelu(x)


def _in_proj_kernel(x_ref, ang_ref, wqkv_ref, wugt_ref, qg_ref, kg_ref, gng_ref, *rest, n_cast):
    cast_in, (q_ref, k_ref, v_ref, ut_ref, gt_ref), cast_out = rest[:n_cast], rest[n_cast:n_cast + 5], rest[n_cast + 5:]
    for src, dst in zip(cast_in, cast_out):
        dst[...] = src[...].astype(BF16)

    xf = x_ref[...]
    ms = jnp.mean(xf * xf, axis=-1, keepdims=True)
    h = (xf * lax.rsqrt(ms + EPS) * ang_ref[...]).astype(BF16)
    tm = xf.shape[0]

    zu = lax.dot_general(wugt_ref[:GM_WIDTH, :], h, _NT, preferred_element_type=F32)
    zg = lax.dot_general(wugt_ref[GM_WIDTH:, :], h, _NT, preferred_element_type=F32)
    ut_ref[...] = _gelu(zu)
    z = jnp.dot(h, wqkv_ref[...], preferred_element_type=F32)
    gl3 = _gelu(zg).reshape(GM_GROUPS, HEAD_DIM, tm)
    msg = jnp.mean(gl3 * gl3, axis=1, keepdims=True)
    gn = gl3 * lax.rsqrt(msg + EPS) * gng_ref[...]
    gt_ref[...] = gn.reshape(GM_WIDTH, tm).astype(BF16)

    low = lax.broadcasted_iota(I32, (1, LANES), 1) < HEAD_DIM

    def two_head_norm(zt, g):
        sq = zt * zt
        s_lo = jnp.sum(jnp.where(low, sq, 0.0), axis=-1, keepdims=True)
        s_hi = jnp.sum(jnp.where(low, 0.0, sq), axis=-1, keepdims=True)
        r = jnp.where(low, lax.rsqrt(s_lo * (1.0 / HEAD_DIM) + EPS), lax.rsqrt(s_hi * (1.0 / HEAD_DIM) + EPS))
        return zt * r * g

    for j in range(ATTN_WIDTH // LANES):
        qn = two_head_norm(z[:, j * LANES:(j + 1) * LANES], qg_ref[...])
        q_ref[:, j * LANES:(j + 1) * LANES] = (qn * Q_SCALE).astype(BF16)
    for j in range(KV_WIDTH // LANES):
        c0 = ATTN_WIDTH + j * LANES
        k_ref[:, j * LANES:(j + 1) * LANES] = two_head_norm(z[:, c0:c0 + LANES], kg_ref[...]).astype(BF16)
    v_ref[...] = z[:, ATTN_WIDTH + KV_WIDTH:].astype(BF16)


def _in_proj(x2d, ang, wqkv, wugt, qg, kg, gng, cast=()):
    t = x2d.shape[0]
    tm = IN_TOKENS
    assert t % tm == 0
    steps = t // tm
    row = lambda i: (i, 0)
    col = lambda i: (0, i)
    cast_specs, cast_shapes = [], []
    for w in cast:
        assert w.shape[0] % steps == 0
        cast_specs.append(pl.BlockSpec((w.shape[0] // steps, w.shape[1]), row))
        cast_shapes.append(jax.ShapeDtypeStruct(w.shape, BF16))
    return pl.pallas_call(
        functools.partial(_in_proj_kernel, n_cast=len(cast)),
        grid=(steps,),
        in_specs=[
            pl.BlockSpec((tm, D_MODEL), row),
            _const_spec((1, D_MODEL)),
            _const_spec((D_MODEL, ATTN_WIDTH + 2 * KV_WIDTH)),
            _const_spec((2 * GM_WIDTH, D_MODEL)),
            _const_spec((1, LANES)),
            _const_spec((1, LANES)),
            _const_spec((GM_GROUPS, HEAD_DIM, 1)),
        ] + cast_specs,
        out_specs=[
            pl.BlockSpec((tm, ATTN_WIDTH), row),
            pl.BlockSpec((tm, KV_WIDTH), row),
            pl.BlockSpec((tm, KV_WIDTH), row),
            pl.BlockSpec((GM_WIDTH, tm), col),
            pl.BlockSpec((GM_WIDTH, tm), col),
        ] + cast_specs,
        out_shape=[
            jax.ShapeDtypeStruct((t, ATTN_WIDTH), BF16),
            jax.ShapeDtypeStruct((t, KV_WIDTH), BF16),
            jax.ShapeDtypeStruct((t, KV_WIDTH), BF16),
            jax.ShapeDtypeStruct((GM_WIDTH, t), F32),
            jax.ShapeDtypeStruct((GM_WIDTH, t), BF16),
        ] + cast_shapes,
        compiler_params=pltpu.CompilerParams(
            dimension_semantics=("arbitrary",), vmem_limit_bytes=VMEM_LIMIT),
        name="in_proj",
    )(x2d, ang, wqkv, wugt, qg, kg, gng, *cast)


def _mixer_kernel(q_ref, kp_ref, kc_ref, kn_ref, vp_ref, vc_ref, vn_ref, ut_ref, gt_ref, x_ref,
                  bias_ref, sink_ref, wsp_ref, bsp_ref, aog_ref, gog_ref, wout_ref, fng_ref,
                  wrs_ref, wrh_ref, x1_ref, xn_ref, aff_ref, *, blocks_per_seq):
    i = pl.program_id(0)
    tb = q_ref.shape[0]
    nsub = tb // CHUNK
    first = (i % blocks_per_seq) == 0
    last = (i % blocks_per_seq) == blocks_per_seq - 1

    kcat = jnp.concatenate([kp_ref[...], kc_ref[...], kn_ref[...]], axis=0)
    vcat = jnp.concatenate([vp_ref[...], vc_ref[...], vn_ref[...]], axis=0)
    key = lax.broadcasted_iota(I32, (1, 3 * CHUNK), 1)
    low = lax.broadcasted_iota(I32, (1, LANES), 1) < HEAD_DIM
    zero = jnp.zeros((), BF16)
    ones_bd = jnp.where((lax.broadcasted_iota(I32, (6 * CHUNK, LANES), 0) < 3 * CHUNK)
                        == (lax.broadcasted_iota(I32, (6 * CHUNK, LANES), 1) < HEAD_DIM), 1.0, 0.0).astype(BF16)

    def swap_halves(t):
        return jnp.concatenate([t[:, HEAD_DIM:], t[:, :HEAD_DIM]], axis=1)

    attn_rows = []
    for sb in range(nsub):
        edge = None
        if sb == 0:
            edge = jnp.where((key < CHUNK) & first, NEG_INF, 0.0)
        if sb == nsub - 1:
            hi = jnp.where((key >= 2 * CHUNK) & last, NEG_INF, 0.0)
            edge = hi if edge is None else edge + hi
        tiles = []
        for jt in range(N_KV_HEADS // 2):
            kt = kcat[sb * CHUNK:sb * CHUNK + 3 * CHUNK, jt * LANES:(jt + 1) * LANES]
            vt = vcat[sb * CHUNK:sb * CHUNK + 3 * CHUNK, jt * LANES:(jt + 1) * LANES]
            kts, vts = swap_halves(kt), swap_halves(vt)
            for jj in range(2):
                j = 2 * jt + jj
                ksame, kswap = (kt, kts) if jj == 0 else (kts, kt)
                vsame, vswap = (vt, vts) if jj == 0 else (vts, vt)
                krhs = jnp.concatenate([jnp.where(low, ksame, zero), jnp.where(low, zero, kswap)], axis=0)
                vrhs = jnp.concatenate([jnp.where(low, vsame, zero), jnp.where(low, zero, vswap)], axis=0)
                vaug = jnp.concatenate([vrhs, ones_bd], axis=1)
                for pr in range(GQA_GROUP // 2):
                    ha = GQA_GROUP * j + 2 * pr
                    qp = q_ref[sb * CHUNK:(sb + 1) * CHUNK, ha * HEAD_DIM:(ha + 2) * HEAD_DIM]
                    s = lax.dot_general(qp, krhs, _NT, preferred_element_type=F32)
                    ps, es = [], []
                    for g in range(2):
                        sh = s[:, g * 3 * CHUNK:(g + 1) * 3 * CHUNK] + bias_ref[ha + g]
                        if edge is not None:
                            sh = sh + edge
                        snk = sink_ref[ha + g]
                        m = jnp.maximum(jnp.max(sh, axis=-1, keepdims=True), snk)
                        ps.append(jnp.exp2(sh - m).astype(BF16))
                        es.append(jnp.exp2(snk - m))
                    o = jnp.dot(jnp.concatenate(ps, axis=1), vaug, preferred_element_type=F32)
                    den = o[:, LANES:] + jnp.where(low, es[0], es[1])
                    tiles.append(o[:, :LANES] / den)
        attn_rows.append(jnp.concatenate(tiles, axis=1))
    attn = jnp.concatenate(attn_rows, axis=0)
    an = attn * lax.rsqrt(jnp.mean(attn * attn, axis=-1, keepdims=True) + EPS) * aog_ref[...]

    st_rows = []
    for pi in range(GM_GROUPS // 2):
        r0 = pi * 2 * HEAD_DIM
        lhs = jnp.concatenate(
            [jnp.concatenate([gt_ref[r0:r0 + HEAD_DIM, c * CHUNK:(c + 1) * CHUNK],
                              gt_ref[r0 + HEAD_DIM:r0 + 2 * HEAD_DIM, c * CHUNK:(c + 1) * CHUNK]], axis=1)
             for c in range(nsub)], axis=0)
        o = jnp.dot(lhs, wsp_ref[pi], preferred_element_type=F32)
        ga = jnp.concatenate([o[c * HEAD_DIM:(c + 1) * HEAD_DIM, :CHUNK] for c in range(nsub)], axis=1)
        gb = jnp.concatenate([o[c * HEAD_DIM:(c + 1) * HEAD_DIM, CHUNK:] for c in range(nsub)], axis=1)
        ba = jnp.concatenate([bsp_ref[2 * pi]] * nsub, axis=1)
        bb = jnp.concatenate([bsp_ref[2 * pi + 1]] * nsub, axis=1)
        st_rows.append(ga + ba)
        st_rows.append(gb + bb)
    gmt = ut_ref[...] * jnp.concatenate(st_rows, axis=0)
    gm = gmt.T
    gn = gm * lax.rsqrt(jnp.mean(gm * gm, axis=-1, keepdims=True) + EPS) * gog_ref[...]

    cat = jnp.concatenate([an.astype(BF16), gn.astype(BF16)], axis=1)
    x1 = x_ref[...] + jnp.dot(cat, wout_ref[...], preferred_element_type=F32)
    x1_ref[...] = x1

    xn = x1 * lax.rsqrt(jnp.mean(x1 * x1, axis=-1, keepdims=True) + EPS) * fng_ref[...]
    xn_ref[...] = xn
    xh = xn.astype(BF16)
    xl = (xn - xh.astype(F32)).astype(BF16)
    r1 = lax.dot_general(wrs_ref[...], xh, _NT, preferred_element_type=F32)
    r2 = lax.dot_general(wrh_ref[...], xl, _NT, preferred_element_type=F32)
    logits = (r1[:N_EXPERTS] + r1[N_EXPERTS:]) + r2
    mx = jnp.max(logits, axis=0, keepdims=True)
    ex = jnp.exp(logits - mx)
    aff_ref[...] = ex / jnp.sum(ex, axis=0, keepdims=True)


def _mixer(q, k, v, ut, gt, x2d, seq_len, consts):
    t = q.shape[0]
    tb = MIX_TOKENS
    assert seq_len % tb == 0 and t % seq_len == 0
    r = tb // CHUNK
    nch = t // CHUNK
    row = lambda i: (i, 0)
    col = lambda i: (0, i)
    prev = lambda i: (jnp.maximum(i * r - 1, 0), 0)
    nxt = lambda i: (jnp.minimum((i + 1) * r, nch - 1), 0)
    (bias, sink, wsp, bsp, aog, gog, wout, fng, wrs, wrh) = consts
    kernel = functools.partial(_mixer_kernel, blocks_per_seq=seq_len // tb)
    return pl.pallas_call(
        kernel,
        grid=(t // tb,),
        in_specs=[
            pl.BlockSpec((tb, ATTN_WIDTH), row),
            pl.BlockSpec((CHUNK, KV_WIDTH), prev),
            pl.BlockSpec((tb, KV_WIDTH), row),
            pl.BlockSpec((CHUNK, KV_WIDTH), nxt),
            pl.BlockSpec((CHUNK, KV_WIDTH), prev),
            pl.BlockSpec((tb, KV_WIDTH), row),
            pl.BlockSpec((CHUNK, KV_WIDTH), nxt),
            pl.BlockSpec((GM_WIDTH, tb), col),
            pl.BlockSpec((GM_WIDTH, tb), col),
            pl.BlockSpec((tb, D_MODEL), row),
            _const_spec((N_Q_HEADS, CHUNK, 3 * CHUNK)),
            pl.BlockSpec(memory_space=pltpu.SMEM),
            _const_spec((GM_GROUPS // 2, 2 * CHUNK, 2 * CHUNK)),
            _const_spec((GM_GROUPS, 1, CHUNK)),
            _const_spec((1, ATTN_WIDTH)),
            _const_spec((1, GM_WIDTH)),
            _const_spec((D_MODEL, D_MODEL)),
            _const_spec((1, D_MODEL)),
            _const_spec((2 * N_EXPERTS, D_MODEL)),
            _const_spec((N_EXPERTS, D_MODEL)),
        ],
        out_specs=[
            pl.BlockSpec((tb, D_MODEL), row),
            pl.BlockSpec((tb, D_MODEL), row),
            pl.BlockSpec((N_EXPERTS, tb), col),
        ],
        out_shape=[
            jax.ShapeDtypeStruct((t, D_MODEL), F32),
            jax.ShapeDtypeStruct((t, D_MODEL), F32),
            jax.ShapeDtypeStruct((N_EXPERTS, t), F32),
        ],
        compiler_params=pltpu.CompilerParams(
            dimension_semantics=("arbitrary",), vmem_limit_bytes=VMEM_LIMIT),
        name="mixer",
    )(q, k, k, k, v, v, v, ut, gt, x2d, bias, sink, wsp, bsp, aog, gog, wout, fng, wrs, wrh)


def _sum12(x):
    return jnp.sum(jnp.sum(x, axis=2, keepdims=True), axis=1, keepdims=True)


def _select_kernel(a_ref, m_ref, mt_ref, at_ref, ppt_ref, ctok_ref, cnt_ref, *, cap, tok_bits):
    ne, nb, _ = a_ref.shape
    a = a_ref[...]
    capf = jnp.float32(cap)

    def thr_step(it, thr):
        cand = thr | jnp.left_shift(jnp.int32(1), 30 - it)
        cnt = _sum12(jnp.where(a >= lax.bitcast_convert_type(cand, F32), 1.0, 0.0))
        return jnp.where(cnt >= capf, cand, thr)

    thr = lax.bitcast_convert_type(lax.fori_loop(0, 31, thr_step, jnp.zeros((ne, 1, 1), I32)), F32)

    gt = a > thr
    eq = a == thr
    need = capf - _sum12(jnp.where(gt, 1.0, 0.0))
    tok = lax.broadcasted_iota(I32, (1, nb, LANES), 1) * LANES + lax.broadcasted_iota(I32, (1, nb, LANES), 2)

    def cut_step(it, c):
        cand = c + jnp.left_shift(jnp.int32(1), tok_bits - 1 - it)
        f = _sum12(jnp.where(eq & (tok < cand), 1.0, 0.0))
        return jnp.where(f < need, cand, c)

    c = lax.fori_loop(0, tok_bits, cut_step, jnp.zeros((ne, 1, 1), I32))
    tcut = jnp.where(need > 0.0, c + 1, 0)

    m = jnp.where(gt | (eq & (tok < tcut)), 1.0, 0.0)
    m_ref[...] = m.astype(BF16)

    tok_t = lax.broadcasted_iota(I32, (LANES, nb), 1) * LANES + lax.broadcasted_iota(I32, (LANES, nb), 0)
    mt = []
    for e in range(ne):
        a_te = a_ref[e].T
        at_ref[e] = a_te
        mt_e = jnp.where((a_te > thr[e]) | ((a_te == thr[e]) & (tok_t < tcut[e])), 1.0, 0.0)
        mt_ref[e] = mt_e.astype(BF16)
        mt.append(mt_e)

    cnt_t = functools.reduce(lambda x, y: x + y, mt)
    rl = lax.broadcasted_iota(I32, (LANES, LANES), 0)
    cl = lax.broadcasted_iota(I32, (LANES, LANES), 1)
    tri_strict_low = jnp.where(cl < rl, 1.0, 0.0).astype(BF16)
    local_t = jnp.dot(tri_strict_low, cnt_t.astype(BF16), preferred_element_type=F32)
    tot_t = jnp.sum(cnt_t, axis=0, keepdims=True)
    rb = lax.broadcasted_iota(I32, (nb, nb), 0)
    cb = lax.broadcasted_iota(I32, (nb, nb), 1)
    blk_before = jnp.where(rb < cb, 1.0, 0.0).astype(BF16)
    pb_t = _exact_dot(jnp.broadcast_to(tot_t, (8, nb)), blk_before)[0:1]
    ctok_t = local_t + pb_t
    rank = jnp.zeros((LANES, nb), F32)
    for e in range(ne):
        ppt_ref[e] = ctok_t + rank
        rank = rank + mt[e]

    cnt_b = jnp.sum(m, axis=0)
    tri_strict_up = jnp.where(rl < cl, 1.0, 0.0).astype(BF16)
    local_b = jnp.dot(cnt_b.astype(BF16), tri_strict_up, preferred_element_type=F32)
    totb = jnp.dot(cnt_b.astype(BF16), jnp.ones((LANES, LANES), BF16), preferred_element_type=F32)
    blk_after = jnp.where(cb < rb, 1.0, 0.0).astype(BF16)
    ctok_ref[...] = local_b + _exact_dot_left(blk_after, totb)
    cnt_ref[...] = cnt_b


def _select(a_blk, cap, tok_bits):
    ne, nb, _ = a_blk.shape
    kernel = functools.partial(_select_kernel, cap=cap, tok_bits=tok_bits)
    full3 = lambda s: pl.BlockSpec(s, lambda: (0, 0, 0))
    full2 = lambda s: pl.BlockSpec(s, lambda: (0, 0))
    return pl.pallas_call(
        kernel,
        in_specs=[full3((ne, nb, LANES))],
        out_specs=[full3((ne, nb, LANES)), full3((ne, LANES, nb)), full3((ne, LANES, nb)), full3((ne, LANES, nb)),
                   full2((nb, LANES)), full2((nb, LANES))],
        out_shape=[
            jax.ShapeDtypeStruct((ne, nb, LANES), BF16),
            jax.ShapeDtypeStruct((ne, LANES, nb), BF16),
            jax.ShapeDtypeStruct((ne, LANES, nb), F32),
            jax.ShapeDtypeStruct((ne, LANES, nb), F32),
            jax.ShapeDtypeStruct((nb, LANES), F32),
            jax.ShapeDtypeStruct((nb, LANES), F32),
        ],
        compiler_params=pltpu.CompilerParams(vmem_limit_bytes=VMEM_LIMIT),
        name="select",
    )(a_blk)


def _compact_kernel(m_ref, mt_ref, at_ref, ppt_ref, idx_ref, gate_ref, pp_ref):
    nb = m_ref.shape[1]
    nchunks = idx_ref.shape[1]
    mb = m_ref[0]
    mt = mt_ref[0]
    a_t = at_ref[0]
    pp_t = ppt_ref[0]

    totb = jnp.dot(mb, jnp.ones((LANES, LANES), BF16), preferred_element_type=F32)
    rb = lax.broadcasted_iota(I32, (nb, nb), 0)
    cb = lax.broadcasted_iota(I32, (nb, nb), 1)
    blk_after = jnp.where(cb < rb, 1.0, 0.0).astype(BF16)
    cb_excl = jnp.dot(blk_after, totb.astype(BF16), preferred_element_type=F32)
    cb_incl = cb_excl + totb
    rl = lax.broadcasted_iota(I32, (LANES, LANES), 0)
    cl = lax.broadcasted_iota(I32, (LANES, LANES), 1)
    tri_low = jnp.where(cl <= rl, 1.0, 0.0).astype(BF16)
    lt = jnp.dot(tri_low, mt, preferred_element_type=F32).astype(BF16)
    sub_b = lax.broadcasted_iota(I32, (nb, LANES), 0).astype(F32)
    sub_l = rl.astype(F32)
    lane = lax.broadcasted_iota(I32, (1, LANES), 1).astype(F32)

    def chunk(c, carry):
        j = lane + lax.convert_element_type(c * LANES, F32)
        bj = jnp.sum(jnp.where(cb_incl <= j, 1.0, 0.0), axis=0, keepdims=True)
        oh = sub_b == bj
        ohb = jnp.where(oh, 1.0, 0.0).astype(BF16)
        jloc = j - jnp.sum(jnp.where(oh, cb_excl, 0.0), axis=0, keepdims=True)
        glt = jnp.dot(lt, ohb, preferred_element_type=F32)
        lj = jnp.sum(jnp.where(glt <= jloc, 1.0, 0.0), axis=0, keepdims=True)
        ohl = sub_l == lj
        gate = jnp.sum(jnp.where(ohl, _exact_dot(a_t, ohb), 0.0), axis=0, keepdims=True)
        pp = jnp.sum(jnp.where(ohl, _exact_dot(pp_t, ohb), 0.0), axis=0, keepdims=True)
        idx_ref[0, pl.ds(c, 1), :] = (bj * LANES + lj).astype(I32)
        gate_ref[0, pl.ds(c, 1), :] = gate
        pp_ref[0, pl.ds(c, 1), :] = pp.astype(I32)
        return carry

    lax.fori_loop(0, nchunks, chunk, 0, unroll=min(4, nchunks))


def _compact(m_blk, m_t, a_t, pp_t, cap):
    ne, nb, _ = m_blk.shape
    assert cap % LANES == 0
    nchunks = cap // LANES
    per_e = lambda s: pl.BlockSpec((1,) + s, lambda e: (e, 0, 0))
    return pl.pallas_call(
        _compact_kernel,
        grid=(ne,),
        in_specs=[per_e((nb, LANES)), per_e((LANES, nb)), per_e((LANES, nb)), per_e((LANES, nb))],
        out_specs=[per_e((nchunks, LANES))] * 3,
        out_shape=[
            jax.ShapeDtypeStruct((ne, nchunks, LANES), I32),
            jax.ShapeDtypeStruct((ne, nchunks, LANES), F32),
            jax.ShapeDtypeStruct((ne, nchunks, LANES), I32),
        ],
        compiler_params=pltpu.CompilerParams(
            dimension_semantics=("arbitrary",), vmem_limit_bytes=VMEM_LIMIT),
        name="compact",
    )(m_blk, m_t, a_t, pp_t)


def _ffn_kernel(idx_ref, idxn_ref, ppp_ref, ppc_ref, gate_ref, x_hbm, wg_ref, wu_ref, wd_ref,
                g_hbm, buf, gsem, ssem, *, tiles_per_expert):
    tm = buf.shape[1]
    step = pl.program_id(0) * tiles_per_expert + pl.program_id(1)
    nsteps = pl.num_programs(0) * tiles_per_expert
    slot = step % 2
    other = 1 - slot

    def gather_copy(row, r, s):
        return pltpu.make_async_copy(x_hbm.at[pl.ds(row, 1), :], buf.at[s, pl.ds(r, 1), :], gsem.at[s])

    def scatter_copy(row, r, s):
        return pltpu.make_async_copy(buf.at[2 + s, pl.ds(r, 1), :], g_hbm.at[pl.ds(row, 1), :], ssem.at[s])

    def wait_gather(s):
        pltpu.make_async_copy(x_hbm.at[pl.ds(0, tm), :], buf.at[s], gsem.at[s]).wait()

    def wait_scatter(s):
        pltpu.make_async_copy(buf.at[2 + s], g_hbm.at[pl.ds(0, tm), :], ssem.at[s]).wait()

    @pl.when(step == 0)
    def _():
        def body(r, carry):
            gather_copy(idx_ref[0, 0, r], r, 0).start()
            return carry
        lax.fori_loop(0, tm, body, 0, unroll=8)
        buf[3] = jnp.zeros((tm, D_MODEL), F32)

    def tile(ps):
        po = 1 - ps

        @pl.when(step > 0)
        def _():
            wait_scatter(ps)

        wait_gather(ps)

        yslot = 2 + slot

        ri = lax.broadcasted_iota(I32, (tm, tm), 0)
        ci = lax.broadcasted_iota(I32, (tm, tm), 1)
        eye = jnp.where(ri == ci, 1.0, 0.0).astype(BF16)
        g1, g2, g3 = _split3(jnp.broadcast_to(gate_ref[0], (8, tm)))
        nt = lambda g: lax.dot_general(eye, g, _NT, preferred_element_type=F32)
        gcol = ((nt(g1) + nt(g2)) + nt(g3))[:, 0:1]

        nchunk = EXPERT_FF // FFN_CHUNK
        nhalf = FFN_ROW_BLOCKS if tm % (16 * FFN_ROW_BLOCKS) == 0 else 1
        hr = tm // nhalf
        ngroup = max(nchunk - 1, 1)
        bounds = [tm * k // ngroup for k in range(ngroup + 1)] + [tm] * (nchunk - ngroup)

        xns = [buf[slot, h * hr:(h + 1) * hr, :].astype(BF16) for h in range(nhalf)]

        def gate_up(h, c):
            cols = slice(c * FFN_CHUNK, (c + 1) * FFN_CHUNK)
            return (jnp.dot(xns[h], wg_ref[0, :, cols], preferred_element_type=F32),
                    jnp.dot(xns[h], wu_ref[0, :, cols], preferred_element_type=F32))

        nxt = [gate_up(h, 0) for h in range(nhalf)]
        for c in range(nchunk):
            for r in range(bounds[c], bounds[c + 1]):
                gather_copy(idxn_ref[0, 0, r], r, po).start()
                scatter_copy(ppp_ref[0, 0, r], r, po).start()
            for h in range(nhalf):
                rows = slice(h * hr, (h + 1) * hr)
                hg, hu = nxt[h]
                if c + 1 < nchunk:
                    nxt[h] = gate_up(h, c + 1)
                hh = (jax.nn.silu(hg) * hu * gcol[rows]).astype(BF16)
                part = jnp.dot(hh, wd_ref[0, c * FFN_CHUNK:(c + 1) * FFN_CHUNK, :], preferred_element_type=F32)
                if c == 0:
                    buf[yslot, rows, :] = part
                else:
                    buf[yslot, rows, :] = buf[yslot, rows, :] + part

    for ps in range(2):
        pl.when(slot == ps)(functools.partial(tile, ps))

    @pl.when(step + 1 == nsteps)
    def _():
        def body(r, carry):
            scatter_copy(ppc_ref[0, 0, r], r, slot).start()
            return carry
        lax.fori_loop(0, tm, body, 0, unroll=8)
        wait_scatter(other)
        wait_scatter(slot)
        wait_gather(other)


def _ffn(idx_t, gate_t, pp_t, xn, wg, wu, wd, n_pairs):
    ntiles, _, tm = idx_t.shape
    tpe = ntiles // N_EXPERTS
    assert tm % (EXPERT_FF // FFN_CHUNK) == 0
    cur = lambda e, i: (e * tpe + i, 0, 0)
    nxt = lambda e, i: (jnp.minimum(e * tpe + i + 1, ntiles - 1), 0, 0)
    prv = lambda e, i: (e * tpe + i, 0, 0)
    cur1 = lambda e, i: (e * tpe + i + 1, 0, 0)
    wmap = lambda e, i: (e, 0, 0)
    spare = (n_pairs + jnp.arange(tm, dtype=I32)).reshape(1, 1, tm)
    pp_ext = jnp.concatenate([spare, pp_t], axis=0)
    kernel = functools.partial(_ffn_kernel, tiles_per_expert=tpe)
    return pl.pallas_call(
        kernel,
        grid=(N_EXPERTS, tpe),
        in_specs=[
            pl.BlockSpec((1, 1, tm), cur, memory_space=pltpu.SMEM),
            pl.BlockSpec((1, 1, tm), nxt, memory_space=pltpu.SMEM),
            pl.BlockSpec((1, 1, tm), prv, memory_space=pltpu.SMEM),
            pl.BlockSpec((1, 1, tm), cur1, memory_space=pltpu.SMEM),
            pl.BlockSpec((1, 1, tm), cur),
            pl.BlockSpec(memory_space=pl.ANY),
            pl.BlockSpec((1, D_MODEL, EXPERT_FF), wmap),
            pl.BlockSpec((1, D_MODEL, EXPERT_FF), wmap),
            pl.BlockSpec((1, EXPERT_FF, D_MODEL), wmap),
        ],
        out_specs=pl.BlockSpec(memory_space=pl.ANY),
        out_shape=jax.ShapeDtypeStruct((n_pairs + tm, D_MODEL), F32),
        scratch_shapes=[
            pltpu.VMEM((4, tm, D_MODEL), F32),
            pltpu.SemaphoreType.DMA((2,)),
            pltpu.SemaphoreType.DMA((2,)),
        ],
        compiler_params=pltpu.CompilerParams(
            dimension_semantics=("arbitrary", "arbitrary"), vmem_limit_bytes=VMEM_LIMIT),
        name="ffn",
    )(idx_t, idx_t, pp_ext, pp_ext, gate_t, xn, wg, wu, wd)


def _combine_kernel(boff_ref, x1_ref, ctok_ref, cnt_ref, g_hbm, y_ref, gbuf, acc_ref, sem, st_ref, *, n_pairs):
    i = pl.program_id(0)
    tb = x1_ref.shape[0]
    nslot, pc, _ = gbuf.shape
    n_chunks = n_pairs // pc
    r = tb // LANES

    def chunk_copy(g):
        s = g % nslot
        return pltpu.make_async_copy(g_hbm.at[pl.ds(g * pc, pc), :], gbuf.at[s], sem.at[s])

    @pl.when(i == 0)
    def _():
        st_ref[0] = 0
        st_ref[1] = 0

    g_lo = boff_ref[i] // pc
    g_hi = (boff_ref[i + 1] + pc - 1) // pc

    ri = lax.broadcasted_iota(I32, (LANES, LANES), 0)
    ci = lax.broadcasted_iota(I32, (LANES, LANES), 1)
    eye = jnp.where(ri == ci, 1.0, 0.0).astype(BF16)

    def to_col(rowv):
        p1, p2, p3 = _split3(jnp.broadcast_to(rowv, (8, LANES)))
        nt = lambda p: lax.dot_general(eye, p, _NT, preferred_element_type=F32)
        return ((nt(p1) + nt(p2)) + nt(p3))[:, 0:1]

    cex_rows = ctok_ref[pl.ds(i * r, r), :]
    cnt_rows = cnt_ref[pl.ds(i * r, r), :]
    cex = jnp.concatenate([to_col(cex_rows[k:k + 1]) for k in range(r)], axis=0)
    cin = cex + jnp.concatenate([to_col(cnt_rows[k:k + 1]) for k in range(r)], axis=0)

    acc_ref[...] = jnp.zeros_like(acc_ref)

    def body(g, carry):
        def start_one(k, c2):
            chunk_copy(k).start()
            return c2
        upto = jnp.minimum(g + nslot, n_chunks)
        lax.fori_loop(st_ref[0], upto, start_one, 0)
        st_ref[0] = jnp.maximum(st_ref[0], upto)

        @pl.when(g >= st_ref[1])
        def _():
            chunk_copy(g).wait()
            st_ref[1] = g + 1

        p = (g * pc + lax.broadcasted_iota(I32, (1, pc), 1)).astype(F32)
        sel = (cex <= p) & (p < cin)
        acc_ref[...] += jnp.dot(jnp.where(sel, 1.0, 0.0).astype(BF16), gbuf[g % nslot].astype(BF16),
                                preferred_element_type=F32)
        return carry

    lax.fori_loop(g_lo, g_hi, body, 0)
    y_ref[...] = x1_ref[...] + acc_ref[...]


def _combine(boff, x1, ctok, cnt, g, n_pairs):
    t = x1.shape[0]
    tb = COMB_TOKENS
    nb = ctok.shape[0]
    assert t % tb == 0 and n_pairs % COMB_PAIRS == 0
    kernel = functools.partial(_combine_kernel, n_pairs=n_pairs)
    grid_spec = pltpu.PrefetchScalarGridSpec(
        num_scalar_prefetch=1,
        grid=(t // tb,),
        in_specs=[
            pl.BlockSpec((tb, D_MODEL), lambda i, b: (i, 0)),
            pl.BlockSpec((nb, LANES), lambda i, b: (0, 0)),
            pl.BlockSpec((nb, LANES), lambda i, b: (0, 0)),
            pl.BlockSpec(memory_space=pl.ANY),
        ],
        out_specs=pl.BlockSpec((tb, D_MODEL), lambda i, b: (i, 0)),
        scratch_shapes=[
            pltpu.VMEM((COMB_SLOTS, COMB_PAIRS, D_MODEL), F32),
            pltpu.VMEM((tb, D_MODEL), F32),
            pltpu.SemaphoreType.DMA((COMB_SLOTS,)),
            pltpu.SMEM((2,), I32),
        ],
    )
    return pl.pallas_call(
        kernel,
        grid_spec=grid_spec,
        out_shape=jax.ShapeDtypeStruct((t, D_MODEL), F32),
        compiler_params=pltpu.CompilerParams(
            dimension_semantics=("arbitrary",), vmem_limit_bytes=VMEM_LIMIT),
        name="combine",
    )(boff, x1, ctok, cnt, g)


def _alibi_bias():
    slopes = 2.0 ** (-8.0 * jnp.arange(1, N_Q_HEADS + 1, dtype=F32) / N_Q_HEADS)
    qi = jnp.arange(CHUNK)[:, None]
    kj = jnp.arange(3 * CHUNK)[None, :]
    rel = jnp.abs(qi + CHUNK - kj)
    bias = -slopes[:, None, None] * rel.astype(F32)[None] * LOG2E
    return jnp.where((rel <= CHUNK)[None], bias, NEG_INF)


def _prep_weights(attn_norm_g, w_in, q_norm_g, k_norm_g, sink, gm_norm_g, w_spatial, b_spatial,
                  attn_out_g, gm_out_g, w_out, ffn_norm_g, w_router, w_gate, w_up, w_down):
    nqkv = ATTN_WIDTH + 2 * KV_WIDTH
    ang = attn_norm_g.reshape(1, D_MODEL)
    wqkv = w_in[:, :nqkv].astype(BF16)
    wugt = w_in[:, nqkv:].T.astype(BF16)
    qg = jnp.tile(q_norm_g, 2).reshape(1, LANES)
    kg = jnp.tile(k_norm_g, 2).reshape(1, LANES)
    gng = gm_norm_g.reshape(GM_GROUPS, HEAD_DIM, 1)
    wst = jnp.swapaxes(w_spatial, 1, 2).astype(BF16)
    zero = jnp.zeros((GM_GROUPS // 2, CHUNK, CHUNK), BF16)
    wsp = jnp.concatenate([jnp.concatenate([wst[0::2], zero], axis=2),
                           jnp.concatenate([zero, wst[1::2]], axis=2)], axis=1)
    bsp = b_spatial.reshape(GM_GROUPS, 1, CHUNK)
    aog = attn_out_g.reshape(1, ATTN_WIDTH)
    gog = gm_out_g.reshape(1, GM_WIDTH)
    wout = w_out.astype(BF16)
    fng = ffn_norm_g.reshape(1, D_MODEL)
    wr_t = w_router.T
    wr_hi = wr_t.astype(BF16)
    wr_lo = (wr_t - wr_hi.astype(F32)).astype(BF16)
    wrs = jnp.concatenate([wr_hi, wr_lo], axis=0)
    mixer_consts = (_alibi_bias(), sink * LOG2E, wsp, bsp, aog, gog, wout, fng, wrs, wr_hi)
    in_consts = (ang, wqkv, wugt, qg, kg, gng)
    return in_consts, mixer_consts


def _cast_plan(weights, steps):
    views = [w.reshape(-1, w.shape[-1]) for w in weights]
    ok = all(v.shape[0] % steps == 0 and (v.shape[0] // steps) * v.shape[1] * 4 <= CAST_SLAB_BYTES for v in views)
    return views if ok else None


def _front(x, in_consts, mixer_consts, cast_weights):
    b, s, d = x.shape
    t = b * s
    x2d = x.reshape(t, d)
    views = _cast_plan(cast_weights, t // IN_TOKENS)
    outs = _in_proj(x2d, *in_consts, cast=tuple(views) if views is not None else ())
    q, k, v, ut, gt = outs[:5]
    if views is not None:
        cast_done = [o.reshape(w.shape) for o, w in zip(outs[5:], cast_weights)]
    else:
        cast_done = [w.astype(BF16) for w in cast_weights]
    x1, xn, aff_t = _mixer(q, k, v, ut, gt, x2d, s, mixer_consts)

    nb = t // LANES
    cap = max(1, CAPACITY_FACTOR * t // N_EXPERTS)
    tok_bits = max(1, (t - 1).bit_length())
    a_blk = aff_t.reshape(N_EXPERTS, nb, LANES)
    m_blk, m_t, a_t, pp_t, ctok, cnt = _select(a_blk, cap, tok_bits)
    idx, gate, pp = _compact(m_blk, m_t, a_t, pp_t, cap)
    return (x1, xn, idx, gate, pp, ctok, cnt, cap), cast_done


def _back(front, wg, wu, wd, shape):
    x1, xn, idx, gate, pp, ctok, cnt, cap = front
    tm = min(FFN_ROWS, cap)
    ntiles = N_EXPERTS * cap // tm
    n_pairs = N_EXPERTS * cap
    g = _ffn(idx.reshape(ntiles, 1, tm), gate.reshape(ntiles, 1, tm), pp.reshape(ntiles, 1, tm),
             xn, wg, wu, wd, n_pairs)
    boff = jnp.concatenate([ctok.reshape(-1)[::COMB_TOKENS].astype(I32), jnp.full((1,), n_pairs, I32)])
    return _combine(boff, x1, ctok, cnt, g, n_pairs).reshape(shape)


def kernel(x_prompt, x_sample, attn_norm_g, w_in, q_norm_g, k_norm_g, sink, gm_norm_g, w_spatial, b_spatial, attn_out_g, gm_out_g, w_out, ffn_norm_g, w_router, w_gate, w_up, w_down):
    y_prompt, y_sample = x_prompt, x_sample
    for l in range(w_in.shape[0]):
        in_consts, mixer_consts = _prep_weights(
            attn_norm_g[l], w_in[l], q_norm_g[l], k_norm_g[l], sink[l], gm_norm_g[l], w_spatial[l], b_spatial[l],
            attn_out_g[l], gm_out_g[l], w_out[l], ffn_norm_g[l], w_router[l], w_gate[l], w_up[l], w_down[l])
        front_p, (wg,) = _front(y_prompt, in_consts, mixer_consts, [w_gate[l]])
        front_s, (wu, wd) = _front(y_sample, in_consts, mixer_consts, [w_up[l], w_down[l]])
        y_prompt = _back(front_p, wg, wu, wd, y_prompt.shape)
        y_sample = _back(front_s, wg, wu, wd, y_sample.shape)
    return (y_prompt, y_sample)
```

```python
import functools

import jax
import jax.numpy as jnp
from jax import lax
from jax.experimental import pallas as pl
from jax.experimental.pallas import tpu as pltpu

F32 = jnp.float32
BF16 = jnp.bfloat16
I32 = jnp.int32

D_MODEL = 2048
HEAD_DIM = 64
ATTN_WIDTH = 1024
KV_WIDTH = 256
GM_WIDTH = 1024
N_Q_HEADS = 16
N_KV_HEADS = 4
GQA_GROUP = 4
GM_GROUPS = 16
CHUNK = 128
N_EXPERTS = 16
EXPERT_FF = 1024
CAPACITY_FACTOR = 2
EPS = 1e-6
NEG_INF = -1e30
LOG2E = 1.4426950408889634
Q_SCALE = HEAD_DIM ** -0.5 * LOG2E

LANES = 128
VMEM_LIMIT = 58 * 1024 * 1024

IN_TOKENS = 512
MIX_TOKENS = 512
FFN_ROWS = 512
FFN_CHUNK = 256
FFN_ROW_BLOCKS = 2
COMB_TOKENS = 512
COMB_PAIRS = 256
COMB_SLOTS = 4
CAST_SLAB_BYTES = 4 * 1024 * 1024

_NT = (((1,), (1,)), ((), ()))


def _const_spec(shape):
    nd = len(shape)
    return pl.BlockSpec(shape, lambda *_: (0,) * nd, pipeline_mode=pl.Buffered(1))


def _split3(a):
    a1 = a.astype(BF16)
    r1 = a - a1.astype(F32)
    a2 = r1.astype(BF16)
    a3 = (r1 - a2.astype(F32)).astype(BF16)
    return a1, a2, a3


def _exact_dot(a, b01):
    a1, a2, a3 = _split3(a)
    d = lambda x: jnp.dot(x, b01, preferred_element_type=F32)
    return (d(a1) + d(a2)) + d(a3)


def _exact_dot_left(a01, b):
    b1, b2, b3 = _split3(b)
    d = lambda x: jnp.dot(a01, x, preferred_element_type=F32)
    return (d(b1) + d(b2)) + d(b3)


def _gelu(x):
    return jax.nn.gelu(x)


def _in_proj_kernel(x_ref, ang_ref, wqkv_ref, wugt_ref, qg_ref, kg_ref, gng_ref, *rest, n_cast):
    cast_in, (q_ref, k_ref, v_ref, ut_ref, gt_ref), cast_out = rest[:n_cast], rest[n_cast:n_cast + 5], rest[n_cast + 5:]
    for src, dst in zip(cast_in, cast_out):
        dst[...] = src[...].astype(BF16)

    xf = x_ref[...]
    ms = jnp.mean(xf * xf, axis=-1, keepdims=True)
    h = (xf * lax.rsqrt(ms + EPS) * ang_ref[...]).astype(BF16)
    tm = xf.shape[0]

    zu = lax.dot_general(wugt_ref[:GM_WIDTH, :], h, _NT, preferred_element_type=F32)
    zg = lax.dot_general(wugt_ref[GM_WIDTH:, :], h, _NT, preferred_element_type=F32)
    ut_ref[...] = _gelu(zu)
    z = jnp.dot(h, wqkv_ref[...], preferred_element_type=F32)
    gl3 = _gelu(zg).reshape(GM_GROUPS, HEAD_DIM, tm)
    msg = jnp.mean(gl3 * gl3, axis=1, keepdims=True)
    gn = gl3 * lax.rsqrt(msg + EPS) * gng_ref[...]
    gt_ref[...] = gn.reshape(GM_WIDTH, tm).astype(BF16)

    low = lax.broadcasted_iota(I32, (1, LANES), 1) < HEAD_DIM

    def two_head_norm(zt, g):
        sq = zt * zt
        s_lo = jnp.sum(jnp.where(low, sq, 0.0), axis=-1, keepdims=True)
        s_hi = jnp.sum(jnp.where(low, 0.0, sq), axis=-1, keepdims=True)
        r = jnp.where(low, lax.rsqrt(s_lo * (1.0 / HEAD_DIM) + EPS), lax.rsqrt(s_hi * (1.0 / HEAD_DIM) + EPS))
        return zt * r * g

    for j in range(ATTN_WIDTH // LANES):
        qn = two_head_norm(z[:, j * LANES:(j + 1) * LANES], qg_ref[...])
        q_ref[:, j * LANES:(j + 1) * LANES] = (qn * Q_SCALE).astype(BF16)
    for j in range(KV_WIDTH // LANES):
        c0 = ATTN_WIDTH + j * LANES
        k_ref[:, j * LANES:(j + 1) * LANES] = two_head_norm(z[:, c0:c0 + LANES], kg_ref[...]).astype(BF16)
    v_ref[...] = z[:, ATTN_WIDTH + KV_WIDTH:].astype(BF16)


def _in_proj(x2d, ang, wqkv, wugt, qg, kg, gng, cast=()):
    t = x2d.shape[0]
    tm = IN_TOKENS
    assert t % tm == 0
    steps = t // tm
    row = lambda i: (i, 0)
    col = lambda i: (0, i)
    cast_specs, cast_shapes = [], []
    for w in cast:
        assert w.shape[0] % steps == 0
        cast_specs.append(pl.BlockSpec((w.shape[0] // steps, w.shape[1]), row))
        cast_shapes.append(jax.ShapeDtypeStruct(w.shape, BF16))
    return pl.pallas_call(
        functools.partial(_in_proj_kernel, n_cast=len(cast)),
        grid=(steps,),
        in_specs=[
            pl.BlockSpec((tm, D_MODEL), row),
            _const_spec((1, D_MODEL)),
            _const_spec((D_MODEL, ATTN_WIDTH + 2 * KV_WIDTH)),
            _const_spec((2 * GM_WIDTH, D_MODEL)),
            _const_spec((1, LANES)),
            _const_spec((1, LANES)),
            _const_spec((GM_GROUPS, HEAD_DIM, 1)),
        ] + cast_specs,
        out_specs=[
            pl.BlockSpec((tm, ATTN_WIDTH), row),
            pl.BlockSpec((tm, KV_WIDTH), row),
            pl.BlockSpec((tm, KV_WIDTH), row),
            pl.BlockSpec((GM_WIDTH, tm), col),
            pl.BlockSpec((GM_WIDTH, tm), col),
        ] + cast_specs,
        out_shape=[
            jax.ShapeDtypeStruct((t, ATTN_WIDTH), BF16),
            jax.ShapeDtypeStruct((t, KV_WIDTH), BF16),
            jax.ShapeDtypeStruct((t, KV_WIDTH), BF16),
            jax.ShapeDtypeStruct((GM_WIDTH, t), F32),
            jax.ShapeDtypeStruct((GM_WIDTH, t), BF16),
        ] + cast_shapes,
        compiler_params=pltpu.CompilerParams(
            dimension_semantics=("arbitrary",), vmem_limit_bytes=VMEM_LIMIT),
        name="in_proj",
    )(x2d, ang, wqkv, wugt, qg, kg, gng, *cast)


def _mixer_kernel(q_ref, kp_ref, kc_ref, kn_ref, vp_ref, vc_ref, vn_ref, ut_ref, gt_ref, x_ref,
                  bias_ref, sink_ref, wsp_ref, bsp_ref, aog_ref, gog_ref, wout_ref, fng_ref,
                  wrs_ref, wrh_ref, x1_ref, xn_ref, aff_ref, *, blocks_per_seq):
    i = pl.program_id(0)
    tb = q_ref.shape[0]
    nsub = tb // CHUNK
    first = (i % blocks_per_seq) == 0
    last = (i % blocks_per_seq) == blocks_per_seq - 1

    kcat = jnp.concatenate([kp_ref[...], kc_ref[...], kn_ref[...]], axis=0)
    vcat = jnp.concatenate([vp_ref[...], vc_ref[...], vn_ref[...]], axis=0)
    key = lax.broadcasted_iota(I32, (1, 3 * CHUNK), 1)
    low = lax.broadcasted_iota(I32, (1, LANES), 1) < HEAD_DIM
    zero = jnp.zeros((), BF16)
    ones_bd = jnp.where((lax.broadcasted_iota(I32, (6 * CHUNK, LANES), 0) < 3 * CHUNK)
                        == (lax.broadcasted_iota(I32, (6 * CHUNK, LANES), 1) < HEAD_DIM), 1.0, 0.0).astype(BF16)

    def swap_halves(t):
        return jnp.concatenate([t[:, HEAD_DIM:], t[:, :HEAD_DIM]], axis=1)

    attn_rows = []
    for sb in range(nsub):
        edge = None
        if sb == 0:
            edge = jnp.where((key < CHUNK) & first, NEG_INF, 0.0)
        if sb == nsub - 1:
            hi = jnp.where((key >= 2 * CHUNK) & last, NEG_INF, 0.0)
            edge = hi if edge is None else edge + hi
        tiles = []
        for jt in range(N_KV_HEADS // 2):
            kt = kcat[sb * CHUNK:sb * CHUNK + 3 * CHUNK, jt * LANES:(jt + 1) * LANES]
            vt = vcat[sb * CHUNK:sb * CHUNK + 3 * CHUNK, jt * LANES:(jt + 1) * LANES]
            kts, vts = swap_halves(kt), swap_halves(vt)
            for jj in range(2):
                j = 2 * jt + jj
                ksame, kswap = (kt, kts) if jj == 0 else (kts, kt)
                vsame, vswap = (vt, vts) if jj == 0 else (vts, vt)
                krhs = jnp.concatenate([jnp.where(low, ksame, zero), jnp.where(low, zero, kswap)], axis=0)
                vrhs = jnp.concatenate([jnp.where(low, vsame, zero), jnp.where(low, zero, vswap)], axis=0)
                vaug = jnp.concatenate([vrhs, ones_bd], axis=1)
                for pr in range(GQA_GROUP // 2):
                    ha = GQA_GROUP * j + 2 * pr
                    qp = q_ref[sb * CHUNK:(sb + 1) * CHUNK, ha * HEAD_DIM:(ha + 2) * HEAD_DIM]
                    s = lax.dot_general(qp, krhs, _NT, preferred_element_type=F32)
                    ps, es = [], []
                    for g in range(2):
                        sh = s[:, g * 3 * CHUNK:(g + 1) * 3 * CHUNK] + bias_ref[ha + g]
                        if edge is not None:
                            sh = sh + edge
                        snk = sink_ref[ha + g]
                        m = jnp.maximum(jnp.max(sh, axis=-1, keepdims=True), snk)
                        ps.append(jnp.exp2(sh - m).astype(BF16))
                        es.append(jnp.exp2(snk - m))
                    o = jnp.dot(jnp.concatenate(ps, axis=1), vaug, preferred_element_type=F32)
                    den = o[:, LANES:] + jnp.where(low, es[0], es[1])
                    tiles.append(o[:, :LANES] / den)
        attn_rows.append(jnp.concatenate(tiles, axis=1))
    attn = jnp.concatenate(attn_rows, axis=0)
    an = attn * lax.rsqrt(jnp.mean(attn * attn, axis=-1, keepdims=True) + EPS) * aog_ref[...]

    st_rows = []
    for pi in range(GM_GROUPS // 2):
        r0 = pi * 2 * HEAD_DIM
        lhs = jnp.concatenate(
            [jnp.concatenate([gt_ref[r0:r0 + HEAD_DIM, c * CHUNK:(c + 1) * CHUNK],
                              gt_ref[r0 + HEAD_DIM:r0 + 2 * HEAD_DIM, c * CHUNK:(c + 1) * CHUNK]], axis=1)
             for c in range(nsub)], axis=0)
        o = jnp.dot(lhs, wsp_ref[pi], preferred_element_type=F32)
        ga = jnp.concatenate([o[c * HEAD_DIM:(c + 1) * HEAD_DIM, :CHUNK] for c in range(nsub)], axis=1)
        gb = jnp.concatenate([o[c * HEAD_DIM:(c + 1) * HEAD_DIM, CHUNK:] for c in range(nsub)], axis=1)
        ba = jnp.concatenate([bsp_ref[2 * pi]] * nsub, axis=1)
        bb = jnp.concatenate([bsp_ref[2 * pi + 1]] * nsub, axis=1)
        st_rows.append(ga + ba)
        st_rows.append(gb + bb)
    gmt = ut_ref[...] * jnp.concatenate(st_rows, axis=0)
    gm = gmt.T
    gn = gm * lax.rsqrt(jnp.mean(gm * gm, axis=-1, keepdims=True) + EPS) * gog_ref[...]

    cat = jnp.concatenate([an.astype(BF16), gn.astype(BF16)], axis=1)
    x1 = x_ref[...] + jnp.dot(cat, wout_ref[...], preferred_element_type=F32)
    x1_ref[...] = x1

    xn = x1 * lax.rsqrt(jnp.mean(x1 * x1, axis=-1, keepdims=True) + EPS) * fng_ref[...]
    xn_ref[...] = xn
    xh = xn.astype(BF16)
    xl = (xn - xh.astype(F32)).astype(BF16)
    r1 = lax.dot_general(wrs_ref[...], xh, _NT, preferred_element_type=F32)
    r2 = lax.dot_general(wrh_ref[...], xl, _NT, preferred_element_type=F32)
    logits = (r1[:N_EXPERTS] + r1[N_EXPERTS:]) + r2
    mx = jnp.max(logits, axis=0, keepdims=True)
    ex = jnp.exp(logits - mx)
    aff_ref[...] = ex / jnp.sum(ex, axis=0, keepdims=True)


def _mixer(q, k, v, ut, gt, x2d, seq_len, consts):
    t = q.shape[0]
    tb = MIX_TOKENS
    assert seq_len % tb == 0 and t % seq_len == 0
    r = tb // CHUNK
    nch = t // CHUNK
    row = lambda i: (i, 0)
    col = lambda i: (0, i)
    prev = lambda i: (jnp.maximum(i * r - 1, 0), 0)
    nxt = lambda i: (jnp.minimum((i + 1) * r, nch - 1), 0)
    (bias, sink, wsp, bsp, aog, gog, wout, fng, wrs, wrh) = consts
    kernel = functools.partial(_mixer_kernel, blocks_per_seq=seq_len // tb)
    return pl.pallas_call(
        kernel,
        grid=(t // tb,),
        in_specs=[
            pl.BlockSpec((tb, ATTN_WIDTH), row),
            pl.BlockSpec((CHUNK, KV_WIDTH), prev),
            pl.BlockSpec((tb, KV_WIDTH), row),
            pl.BlockSpec((CHUNK, KV_WIDTH), nxt),
            pl.BlockSpec((CHUNK, KV_WIDTH), prev),
            pl.BlockSpec((tb, KV_WIDTH), row),
            pl.BlockSpec((CHUNK, KV_WIDTH), nxt),
            pl.BlockSpec((GM_WIDTH, tb), col),
            pl.BlockSpec((GM_WIDTH, tb), col),
            pl.BlockSpec((tb, D_MODEL), row),
            _const_spec((N_Q_HEADS, CHUNK, 3 * CHUNK)),
            pl.BlockSpec(memory_space=pltpu.SMEM),
            _const_spec((GM_GROUPS // 2, 2 * CHUNK, 2 * CHUNK)),
            _const_spec((GM_GROUPS, 1, CHUNK)),
            _const_spec((1, ATTN_WIDTH)),
            _const_spec((1, GM_WIDTH)),
            _const_spec((D_MODEL, D_MODEL)),
            _const_spec((1, D_MODEL)),
            _const_spec((2 * N_EXPERTS, D_MODEL)),
            _const_spec((N_EXPERTS, D_MODEL)),
        ],
        out_specs=[
            pl.BlockSpec((tb, D_MODEL), row),
            pl.BlockSpec((tb, D_MODEL), row),
            pl.BlockSpec((N_EXPERTS, tb), col),
        ],
        out_shape=[
            jax.ShapeDtypeStruct((t, D_MODEL), F32),
            jax.ShapeDtypeStruct((t, D_MODEL), F32),
            jax.ShapeDtypeStruct((N_EXPERTS, t), F32),
        ],
        compiler_params=pltpu.CompilerParams(
            dimension_semantics=("arbitrary",), vmem_limit_bytes=VMEM_LIMIT),
        name="mixer",
    )(q, k, k, k, v, v, v, ut, gt, x2d, bias, sink, wsp, bsp, aog, gog, wout, fng, wrs, wrh)


def _sum12(x):
    return jnp.sum(jnp.sum(x, axis=2, keepdims=True), axis=1, keepdims=True)


def _select_kernel(a_ref, m_ref, mt_ref, at_ref, ppt_ref, ctok_ref, cnt_ref, *, cap, tok_bits):
    ne, nb, _ = a_ref.shape
    a = a_ref[...]
    capf = jnp.float32(cap)

    def thr_step(it, thr):
        cand = thr | jnp.left_shift(jnp.int32(1), 30 - it)
        cnt = _sum12(jnp.where(a >= lax.bitcast_convert_type(cand, F32), 1.0, 0.0))
        return jnp.where(cnt >= capf, cand, thr)

    thr = lax.bitcast_convert_type(lax.fori_loop(0, 31, thr_step, jnp.zeros((ne, 1, 1), I32)), F32)

    gt = a > thr
    eq = a == thr
    need = capf - _sum12(jnp.where(gt, 1.0, 0.0))
    tok = lax.broadcasted_iota(I32, (1, nb, LANES), 1) * LANES + lax.broadcasted_iota(I32, (1, nb, LANES), 2)

    def cut_step(it, c):
        cand = c + jnp.left_shift(jnp.int32(1), tok_bits - 1 - it)
        f = _sum12(jnp.where(eq & (tok < cand), 1.0, 0.0))
        return jnp.where(f < need, cand, c)

    c = lax.fori_loop(0, tok_bits, cut_step, jnp.zeros((ne, 1, 1), I32))
    tcut = jnp.where(need > 0.0, c + 1, 0)

    m = jnp.where(gt | (eq & (tok < tcut)), 1.0, 0.0)
    m_ref[...] = m.astype(BF16)

    tok_t = lax.broadcasted_iota(I32, (LANES, nb), 1) * LANES + lax.broadcasted_iota(I32, (LANES, nb), 0)
    mt = []
    for e in range(ne):
        a_te = a_ref[e].T
        at_ref[e] = a_te
        mt_e = jnp.where((a_te > thr[e]) | ((a_te == thr[e]) & (tok_t < tcut[e])), 1.0, 0.0)
        mt_ref[e] = mt_e.astype(BF16)
        mt.append(mt_e)

    cnt_t = functools.reduce(lambda x, y: x + y, mt)
    rl = lax.broadcasted_iota(I32, (LANES, LANES), 0)
    cl = lax.broadcasted_iota(I32, (LANES, LANES), 1)
    tri_strict_low = jnp.where(cl < rl, 1.0, 0.0).astype(BF16)
    local_t = jnp.dot(tri_strict_low, cnt_t.astype(BF16), preferred_element_type=F32)
    tot_t = jnp.sum(cnt_t, axis=0, keepdims=True)
    rb = lax.broadcasted_iota(I32, (nb, nb), 0)
    cb = lax.broadcasted_iota(I32, (nb, nb), 1)
    blk_before = jnp.where(rb < cb, 1.0, 0.0).astype(BF16)
    pb_t = _exact_dot(jnp.broadcast_to(tot_t, (8, nb)), blk_before)[0:1]
    ctok_t = local_t + pb_t
    rank = jnp.zeros((LANES, nb), F32)
    for e in range(ne):
        ppt_ref[e] = ctok_t + rank
        rank = rank + mt[e]

    cnt_b = jnp.sum(m, axis=0)
    tri_strict_up = jnp.where(rl < cl, 1.0, 0.0).astype(BF16)
    local_b = jnp.dot(cnt_b.astype(BF16), tri_strict_up, preferred_element_type=F32)
    totb = jnp.dot(cnt_b.astype(BF16), jnp.ones((LANES, LANES), BF16), preferred_element_type=F32)
    blk_after = jnp.where(cb < rb, 1.0, 0.0).astype(BF16)
    ctok_ref[...] = local_b + _exact_dot_left(blk_after, totb)
    cnt_ref[...] = cnt_b


def _select(a_blk, cap, tok_bits):
    ne, nb, _ = a_blk.shape
    kernel = functools.partial(_select_kernel, cap=cap, tok_bits=tok_bits)
    full3 = lambda s: pl.BlockSpec(s, lambda: (0, 0, 0))
    full2 = lambda s: pl.BlockSpec(s, lambda: (0, 0))
    return pl.pallas_call(
        kernel,
        in_specs=[full3((ne, nb, LANES))],
        out_specs=[full3((ne, nb, LANES)), full3((ne, LANES, nb)), full3((ne, LANES, nb)), full3((ne, LANES, nb)),
                   full2((nb, LANES)), full2((nb, LANES))],
        out_shape=[
            jax.ShapeDtypeStruct((ne, nb, LANES), BF16),
            jax.ShapeDtypeStruct((ne, LANES, nb), BF16),
            jax.ShapeDtypeStruct((ne, LANES, nb), F32),
            jax.ShapeDtypeStruct((ne, LANES, nb), F32),
            jax.ShapeDtypeStruct((nb, LANES), F32),
            jax.ShapeDtypeStruct((nb, LANES), F32),
        ],
        compiler_params=pltpu.CompilerParams(vmem_limit_bytes=VMEM_LIMIT),
        name="select",
    )(a_blk)


def _compact_kernel(m_ref, mt_ref, at_ref, ppt_ref, idx_ref, gate_ref, pp_ref):
    nb = m_ref.shape[1]
    nchunks = idx_ref.shape[1]
    mb = m_ref[0]
    mt = mt_ref[0]
    a_t = at_ref[0]
    pp_t = ppt_ref[0]

    totb = jnp.dot(mb, jnp.ones((LANES, LANES), BF16), preferred_element_type=F32)
    rb = lax.broadcasted_iota(I32, (nb, nb), 0)
    cb = lax.broadcasted_iota(I32, (nb, nb), 1)
    blk_after = jnp.where(cb < rb, 1.0, 0.0).astype(BF16)
    cb_excl = jnp.dot(blk_after, totb.astype(BF16), preferred_element_type=F32)
    cb_incl = cb_excl + totb
    rl = lax.broadcasted_iota(I32, (LANES, LANES), 0)
    cl = lax.broadcasted_iota(I32, (LANES, LANES), 1)
    tri_low = jnp.where(cl <= rl, 1.0, 0.0).astype(BF16)
    lt = jnp.dot(tri_low, mt, preferred_element_type=F32).astype(BF16)
    sub_b = lax.broadcasted_iota(I32, (nb, LANES), 0).astype(F32)
    sub_l = rl.astype(F32)
    lane = lax.broadcasted_iota(I32, (1, LANES), 1).astype(F32)

    def chunk(c, carry):
        j = lane + lax.convert_element_type(c * LANES, F32)
        bj = jnp.sum(jnp.where(cb_incl <= j, 1.0, 0.0), axis=0, keepdims=True)
        oh = sub_b == bj
        ohb = jnp.where(oh, 1.0, 0.0).astype(BF16)
        jloc = j - jnp.sum(jnp.where(oh, cb_excl, 0.0), axis=0, keepdims=True)
        glt = jnp.dot(lt, ohb, preferred_element_type=F32)
        lj = jnp.sum(jnp.where(glt <= jloc, 1.0, 0.0), axis=0, keepdims=True)
        ohl = sub_l == lj
        gate = jnp.sum(jnp.where(ohl, _exact_dot(a_t, ohb), 0.0), axis=0, keepdims=True)
        pp = jnp.sum(jnp.where(ohl, _exact_dot(pp_t, ohb), 0.0), axis=0, keepdims=True)
        idx_ref[0, pl.ds(c, 1), :] = (bj * LANES + lj).astype(I32)
        gate_ref[0, pl.ds(c, 1), :] = gate
        pp_ref[0, pl.ds(c, 1), :] = pp.astype(I32)
        return carry

    lax.fori_loop(0, nchunks, chunk, 0, unroll=min(4, nchunks))


def _compact(m_blk, m_t, a_t, pp_t, cap):
    ne, nb, _ = m_blk.shape
    assert cap % LANES == 0
    nchunks = cap // LANES
    per_e = lambda s: pl.BlockSpec((1,) + s, lambda e: (e, 0, 0))
    return pl.pallas_call(
        _compact_kernel,
        grid=(ne,),
        in_specs=[per_e((nb, LANES)), per_e((LANES, nb)), per_e((LANES, nb)), per_e((LANES, nb))],
        out_specs=[per_e((nchunks, LANES))] * 3,
        out_shape=[
            jax.ShapeDtypeStruct((ne, nchunks, LANES), I32),
            jax.ShapeDtypeStruct((ne, nchunks, LANES), F32),
            jax.ShapeDtypeStruct((ne, nchunks, LANES), I32),
        ],
        compiler_params=pltpu.CompilerParams(
            dimension_semantics=("arbitrary",), vmem_limit_bytes=VMEM_LIMIT),
        name="compact",
    )(m_blk, m_t, a_t, pp_t)


def _ffn_kernel(idx_ref, idxn_ref, ppp_ref, ppc_ref, gate_ref, x_hbm, wg_ref, wu_ref, wd_ref,
                g_hbm, buf, gsem, ssem, *, tiles_per_expert):
    tm = buf.shape[1]
    step = pl.program_id(0) * tiles_per_expert + pl.program_id(1)
    nsteps = pl.num_programs(0) * tiles_per_expert
    slot = step % 2
    other = 1 - slot

    def gather_copy(row, r, s):
        return pltpu.make_async_copy(x_hbm.at[pl.ds(row, 1), :], buf.at[s, pl.ds(r, 1), :], gsem.at[s])

    def scatter_copy(row, r, s):
        return pltpu.make_async_copy(buf.at[2 + s, pl.ds(r, 1), :], g_hbm.at[pl.ds(row, 1), :], ssem.at[s])

    def wait_gather(s):
        pltpu.make_async_copy(x_hbm.at[pl.ds(0, tm), :], buf.at[s], gsem.at[s]).wait()

    def wait_scatter(s):
        pltpu.make_async_copy(buf.at[2 + s], g_hbm.at[pl.ds(0, tm), :], ssem.at[s]).wait()

    @pl.when(step == 0)
    def _():
        def body(r, carry):
            gather_copy(idx_ref[0, 0, r], r, 0).start()
            return carry
        lax.fori_loop(0, tm, body, 0, unroll=8)
        buf[3] = jnp.zeros((tm, D_MODEL), F32)

    def tile(ps):
        po = 1 - ps

        @pl.when(step > 0)
        def _():
            wait_scatter(ps)

        wait_gather(ps)

        yslot = 2 + slot

        gw = LANES if tm % LANES == 0 else tm
        ri = lax.broadcasted_iota(I32, (gw, gw), 0)
        ci = lax.broadcasted_iota(I32, (gw, gw), 1)
        eye = jnp.where(ri == ci, 1.0, 0.0).astype(BF16)
        g1, g2, g3 = _split3(jnp.broadcast_to(gate_ref[0], (8, tm)))
        nt = lambda g: lax.dot_general(eye, g, _NT, preferred_element_type=F32)
        gcol = jnp.concatenate(
            [((nt(g1[:, b:b + gw]) + nt(g2[:, b:b + gw])) + nt(g3[:, b:b + gw]))[:, 0:1] for b in range(0, tm, gw)],
            axis=0)

        nchunk = EXPERT_FF // FFN_CHUNK
        nhalf = FFN_ROW_BLOCKS if tm % (16 * FFN_ROW_BLOCKS) == 0 else 1
        hr = tm // nhalf
        ngroup = max(nchunk - 1, 1)
        bounds = [tm * k // ngroup for k in range(ngroup + 1)] + [tm] * (nchunk - ngroup)

        xns = [buf[slot, h * hr:(h + 1) * hr, :].astype(BF16) for h in range(nhalf)]

        def gate_up(h, c):
            cols = slice(c * FFN_CHUNK, (c + 1) * FFN_CHUNK)
            return (jnp.dot(xns[h], wg_ref[0, :, cols], preferred_element_type=F32),
                    jnp.dot(xns[h], wu_ref[0, :, cols], preferred_element_type=F32))

        nxt = [gate_up(h, 0) for h in range(nhalf)]
        for c in range(nchunk):
            for r in range(bounds[c], bounds[c + 1]):
                gather_copy(idxn_ref[0, 0, r], r, po).start()
                scatter_copy(ppp_ref[0, 0, r], r, po).start()
            for h in range(nhalf):
                rows = slice(h * hr, (h + 1) * hr)
                hg, hu = nxt[h]
                if c + 1 < nchunk:
                    nxt[h] = gate_up(h, c + 1)
                hh = (jax.nn.silu(hg) * hu * gcol[rows]).astype(BF16)
                part = jnp.dot(hh, wd_ref[0, c * FFN_CHUNK:(c + 1) * FFN_CHUNK, :], preferred_element_type=F32)
                if c == 0:
                    buf[yslot, rows, :] = part
                else:
                    buf[yslot, rows, :] = buf[yslot, rows, :] + part

    for ps in range(2):
        pl.when(slot == ps)(functools.partial(tile, ps))

    @pl.when(step + 1 == nsteps)
    def _():
        def body(r, carry):
            scatter_copy(ppc_ref[0, 0, r], r, slot).start()
            return carry
        lax.fori_loop(0, tm, body, 0, unroll=8)
        wait_scatter(other)
        wait_scatter(slot)
        wait_gather(other)


def _ffn(idx_t, gate_t, pp_t, xn, wg, wu, wd, n_pairs):
    ntiles, _, tm = idx_t.shape
    tpe = ntiles // N_EXPERTS
    assert tm % (EXPERT_FF // FFN_CHUNK) == 0
    cur = lambda e, i: (e * tpe + i, 0, 0)
    nxt = lambda e, i: (jnp.minimum(e * tpe + i + 1, ntiles - 1), 0, 0)
    prv = lambda e, i: (e * tpe + i, 0, 0)
    cur1 = lambda e, i: (e * tpe + i + 1, 0, 0)
    wmap = lambda e, i: (e, 0, 0)
    spare = (n_pairs + jnp.arange(tm, dtype=I32)).reshape(1, 1, tm)
    pp_ext = jnp.concatenate([spare, pp_t], axis=0)
    kernel = functools.partial(_ffn_kernel, tiles_per_expert=tpe)
    return pl.pallas_call(
        kernel,
        grid=(N_EXPERTS, tpe),
        in_specs=[
            pl.BlockSpec((1, 1, tm), cur, memory_space=pltpu.SMEM),
            pl.BlockSpec((1, 1, tm), nxt, memory_space=pltpu.SMEM),
            pl.BlockSpec((1, 1, tm), prv, memory_space=pltpu.SMEM),
            pl.BlockSpec((1, 1, tm), cur1, memory_space=pltpu.SMEM),
            pl.BlockSpec((1, 1, tm), cur),
            pl.BlockSpec(memory_space=pl.ANY),
            pl.BlockSpec((1, D_MODEL, EXPERT_FF), wmap),
            pl.BlockSpec((1, D_MODEL, EXPERT_FF), wmap),
            pl.BlockSpec((1, EXPERT_FF, D_MODEL), wmap),
        ],
        out_specs=pl.BlockSpec(memory_space=pl.ANY),
        out_shape=jax.ShapeDtypeStruct((n_pairs + tm, D_MODEL), F32),
        scratch_shapes=[
            pltpu.VMEM((4, tm, D_MODEL), F32),
            pltpu.SemaphoreType.DMA((2,)),
            pltpu.SemaphoreType.DMA((2,)),
        ],
        compiler_params=pltpu.CompilerParams(
            dimension_semantics=("arbitrary", "arbitrary"), vmem_limit_bytes=VMEM_LIMIT),
        name="ffn",
    )(idx_t, idx_t, pp_ext, pp_ext, gate_t, xn, wg, wu, wd)


def _combine_kernel(boff_ref, x1_ref, ctok_ref, cnt_ref, g_hbm, y_ref, gbuf, acc_ref, sem, st_ref, *, n_pairs):
    i = pl.program_id(0)
    tb = x1_ref.shape[0]
    nslot, pc, _ = gbuf.shape
    n_chunks = n_pairs // pc
    r = tb // LANES

    def chunk_copy(g):
        s = g % nslot
        return pltpu.make_async_copy(g_hbm.at[pl.ds(g * pc, pc), :], gbuf.at[s], sem.at[s])

    @pl.when(i == 0)
    def _():
        st_ref[0] = 0
        st_ref[1] = 0

    g_lo = boff_ref[i] // pc
    g_hi = (boff_ref[i + 1] + pc - 1) // pc

    ri = lax.broadcasted_iota(I32, (LANES, LANES), 0)
    ci = lax.broadcasted_iota(I32, (LANES, LANES), 1)
    eye = jnp.where(ri == ci, 1.0, 0.0).astype(BF16)

    def to_col(rowv):
        p1, p2, p3 = _split3(jnp.broadcast_to(rowv, (8, LANES)))
        nt = lambda p: lax.dot_general(eye, p, _NT, preferred_element_type=F32)
        return ((nt(p1) + nt(p2)) + nt(p3))[:, 0:1]

    cex_rows = ctok_ref[pl.ds(i * r, r), :]
    cnt_rows = cnt_ref[pl.ds(i * r, r), :]
    cex = jnp.concatenate([to_col(cex_rows[k:k + 1]) for k in range(r)], axis=0)
    cin = cex + jnp.concatenate([to_col(cnt_rows[k:k + 1]) for k in range(r)], axis=0)

    acc_ref[...] = jnp.zeros_like(acc_ref)

    def body(g, carry):
        def start_one(k, c2):
            chunk_copy(k).start()
            return c2
        upto = jnp.minimum(g + nslot, n_chunks)
        lax.fori_loop(st_ref[0], upto, start_one, 0)
        st_ref[0] = jnp.maximum(st_ref[0], upto)

        @pl.when(g >= st_ref[1])
        def _():
            chunk_copy(g).wait()
            st_ref[1] = g + 1

        p = (g * pc + lax.broadcasted_iota(I32, (1, pc), 1)).astype(F32)
        sel = (cex <= p) & (p < cin)
        acc_ref[...] += jnp.dot(jnp.where(sel, 1.0, 0.0).astype(BF16), gbuf[g % nslot].astype(BF16),
                                preferred_element_type=F32)
        return carry

    lax.fori_loop(g_lo, g_hi, body, 0)
    y_ref[...] = x1_ref[...] + acc_ref[...]


def _combine(boff, x1, ctok, cnt, g, n_pairs):
    t = x1.shape[0]
    tb = COMB_TOKENS
    nb = ctok.shape[0]
    assert t % tb == 0 and n_pairs % COMB_PAIRS == 0
    kernel = functools.partial(_combine_kernel, n_pairs=n_pairs)
    grid_spec = pltpu.PrefetchScalarGridSpec(
        num_scalar_prefetch=1,
        grid=(t // tb,),
        in_specs=[
            pl.BlockSpec((tb, D_MODEL), lambda i, b: (i, 0)),
            pl.BlockSpec((nb, LANES), lambda i, b: (0, 0)),
            pl.BlockSpec((nb, LANES), lambda i, b: (0, 0)),
            pl.BlockSpec(memory_space=pl.ANY),
        ],
        out_specs=pl.BlockSpec((tb, D_MODEL), lambda i, b: (i, 0)),
        scratch_shapes=[
            pltpu.VMEM((COMB_SLOTS, COMB_PAIRS, D_MODEL), F32),
            pltpu.VMEM((tb, D_MODEL), F32),
            pltpu.SemaphoreType.DMA((COMB_SLOTS,)),
            pltpu.SMEM((2,), I32),
        ],
    )
    return pl.pallas_call(
        kernel,
        grid_spec=grid_spec,
        out_shape=jax.ShapeDtypeStruct((t, D_MODEL), F32),
        compiler_params=pltpu.CompilerParams(
            dimension_semantics=("arbitrary",), vmem_limit_bytes=VMEM_LIMIT),
        name="combine",
    )(boff, x1, ctok, cnt, g)


def _alibi_bias():
    slopes = 2.0 ** (-8.0 * jnp.arange(1, N_Q_HEADS + 1, dtype=F32) / N_Q_HEADS)
    qi = jnp.arange(CHUNK)[:, None]
    kj = jnp.arange(3 * CHUNK)[None, :]
    rel = jnp.abs(qi + CHUNK - kj)
    bias = -slopes[:, None, None] * rel.astype(F32)[None] * LOG2E
    return jnp.where((rel <= CHUNK)[None], bias, NEG_INF)


def _prep_weights(attn_norm_g, w_in, q_norm_g, k_norm_g, sink, gm_norm_g, w_spatial, b_spatial,
                  attn_out_g, gm_out_g, w_out, ffn_norm_g, w_router, w_gate, w_up, w_down):
    nqkv = ATTN_WIDTH + 2 * KV_WIDTH
    ang = attn_norm_g.reshape(1, D_MODEL)
    wqkv = w_in[:, :nqkv].astype(BF16)
    wugt = w_in[:, nqkv:].T.astype(BF16)
    qg = jnp.tile(q_norm_g, 2).reshape(1, LANES)
    kg = jnp.tile(k_norm_g, 2).reshape(1, LANES)
    gng = gm_norm_g.reshape(GM_GROUPS, HEAD_DIM, 1)
    wst = jnp.swapaxes(w_spatial, 1, 2).astype(BF16)
    zero = jnp.zeros((GM_GROUPS // 2, CHUNK, CHUNK), BF16)
    wsp = jnp.concatenate([jnp.concatenate([wst[0::2], zero], axis=2),
                           jnp.concatenate([zero, wst[1::2]], axis=2)], axis=1)
    bsp = b_spatial.reshape(GM_GROUPS, 1, CHUNK)
    aog = attn_out_g.reshape(1, ATTN_WIDTH)
    gog = gm_out_g.reshape(1, GM_WIDTH)
    wout = w_out.astype(BF16)
    fng = ffn_norm_g.reshape(1, D_MODEL)
    wr_t = w_router.T
    wr_hi = wr_t.astype(BF16)
    wr_lo = (wr_t - wr_hi.astype(F32)).astype(BF16)
    wrs = jnp.concatenate([wr_hi, wr_lo], axis=0)
    mixer_consts = (_alibi_bias(), sink * LOG2E, wsp, bsp, aog, gog, wout, fng, wrs, wr_hi)
    in_consts = (ang, wqkv, wugt, qg, kg, gng)
    return in_consts, mixer_consts


def _cast_plan(weights, steps):
    views = [w.reshape(-1, w.shape[-1]) for w in weights]
    ok = all(v.shape[0] % steps == 0 and (v.shape[0] // steps) * v.shape[1] * 4 <= CAST_SLAB_BYTES for v in views)
    return views if ok else None


def _front(x, in_consts, mixer_consts, cast_weights):
    b, s, d = x.shape
    t = b * s
    x2d = x.reshape(t, d)
    views = _cast_plan(cast_weights, t // IN_TOKENS)
    outs = _in_proj(x2d, *in_consts, cast=tuple(views) if views is not None else ())
    q, k, v, ut, gt = outs[:5]
    if views is not None:
        cast_done = [o.reshape(w.shape) for o, w in zip(outs[5:], cast_weights)]
    else:
        cast_done = [w.astype(BF16) for w in cast_weights]
    x1, xn, aff_t = _mixer(q, k, v, ut, gt, x2d, s, mixer_consts)

    nb = t // LANES
    cap = max(1, CAPACITY_FACTOR * t // N_EXPERTS)
    tok_bits = max(1, (t - 1).bit_length())
    a_blk = aff_t.reshape(N_EXPERTS, nb, LANES)
    m_blk, m_t, a_t, pp_t, ctok, cnt = _select(a_blk, cap, tok_bits)
    idx, gate, pp = _compact(m_blk, m_t, a_t, pp_t, cap)
    return (x1, xn, idx, gate, pp, ctok, cnt, cap), cast_done


def _back(front, wg, wu, wd, shape):
    x1, xn, idx, gate, pp, ctok, cnt, cap = front
    tm = min(FFN_ROWS, cap)
    ntiles = N_EXPERTS * cap // tm
    n_pairs = N_EXPERTS * cap
    g = _ffn(idx.reshape(ntiles, 1, tm), gate.reshape(ntiles, 1, tm), pp.reshape(ntiles, 1, tm),
             xn, wg, wu, wd, n_pairs)
    boff = jnp.concatenate([ctok.reshape(-1)[::COMB_TOKENS].astype(I32), jnp.full((1,), n_pairs, I32)])
    return _combine(boff, x1, ctok, cnt, g, n_pairs).reshape(shape)


def kernel(x_prompt, x_sample, attn_norm_g, w_in, q_norm_g, k_norm_g, sink, gm_norm_g, w_spatial, b_spatial, attn_out_g, gm_out_g, w_out, ffn_norm_g, w_router, w_gate, w_up, w_down):
    y_prompt, y_sample = x_prompt, x_sample
    for l in range(w_in.shape[0]):
        in_consts, mixer_consts = _prep_weights(
            attn_norm_g[l], w_in[l], q_norm_g[l], k_norm_g[l], sink[l], gm_norm_g[l], w_spatial[l], b_spatial[l],
            attn_out_g[l], gm_out_g[l], w_out[l], ffn_norm_g[l], w_router[l], w_gate[l], w_up[l], w_down[l])
        front_p, (wg,) = _front(y_prompt, in_consts, mixer_consts, [w_gate[l]])
        front_s, (wu, wd) = _front(y_sample, in_consts, mixer_consts, [w_up[l], w_down[l]])
        y_prompt = _back(front_p, wg, wu, wd, y_prompt.shape)
        y_sample = _back(front_s, wg, wu, wd, y_sample.shape)
    return (y_prompt, y_sample)
```

```python
import functools

import jax
import jax.numpy as jnp
from jax import lax
from jax.experimental import pallas as pl
from jax.experimental.pallas import tpu as pltpu

F32 = jnp.float32
BF16 = jnp.bfloat16
I32 = jnp.int32

D_MODEL = 2048
HEAD_DIM = 64
ATTN_WIDTH = 1024
KV_WIDTH = 256
GM_WIDTH = 1024
N_Q_HEADS = 16
N_KV_HEADS = 4
GQA_GROUP = 4
GM_GROUPS = 16
CHUNK = 128
N_EXPERTS = 16
EXPERT_FF = 1024
CAPACITY_FACTOR = 2
EPS = 1e-6
NEG_INF = -1e30
LOG2E = 1.4426950408889634
Q_SCALE = HEAD_DIM ** -0.5 * LOG2E

LANES = 128
VMEM_LIMIT = 58 * 1024 * 1024

IN_TOKENS = 512
MIX_TOKENS = 512
FFN_ROWS = 512
FFN_CHUNK = 512
FFN_ROW_BLOCKS = 2
COMB_TOKENS = 512
COMB_PAIRS = 256
COMB_SLOTS = 4
CAST_SLAB_BYTES = 4 * 1024 * 1024

_NT = (((1,), (1,)), ((), ()))


def _const_spec(shape):
    nd = len(shape)
    return pl.BlockSpec(shape, lambda *_: (0,) * nd, pipeline_mode=pl.Buffered(1))


def _split3(a):
    a1 = a.astype(BF16)
    r1 = a - a1.astype(F32)
    a2 = r1.astype(BF16)
    a3 = (r1 - a2.astype(F32)).astype(BF16)
    return a1, a2, a3


def _exact_dot(a, b01):
    a1, a2, a3 = _split3(a)
    d = lambda x: jnp.dot(x, b01, preferred_element_type=F32)
    return (d(a1) + d(a2)) + d(a3)


def _exact_dot_left(a01, b):
    b1, b2, b3 = _split3(b)
    d = lambda x: jnp.dot(a01, x, preferred_element_type=F32)
    return (d(b1) + d(b2)) + d(b3)


def _gelu(x):
    return jax.nn.gelu(x)


def _in_proj_kernel(x_ref, ang_ref, wqkv_ref, wugt_ref, qg_ref, kg_ref, gng_ref, *rest, n_cast):
    cast_in, (q_ref, k_ref, v_ref, ut_ref, gt_ref), cast_out = rest[:n_cast], rest[n_cast:n_cast + 5], rest[n_cast + 5:]
    for src, dst in zip(cast_in, cast_out):
        dst[...] = src[...].astype(BF16)

    xf = x_ref[...]
    ms = jnp.mean(xf * xf, axis=-1, keepdims=True)
    h = (xf * lax.rsqrt(ms + EPS) * ang_ref[...]).astype(BF16)
    tm = xf.shape[0]

    zu = lax.dot_general(wugt_ref[:GM_WIDTH, :], h, _NT, preferred_element_type=F32)
    zg = lax.dot_general(wugt_ref[GM_WIDTH:, :], h, _NT, preferred_element_type=F32)
    ut_ref[...] = _gelu(zu)
    z = jnp.dot(h, wqkv_ref[...], preferred_element_type=F32)
    gl3 = _gelu(zg).reshape(GM_GROUPS, HEAD_DIM, tm)
    msg = jnp.mean(gl3 * gl3, axis=1, keepdims=True)
    gn = gl3 * lax.rsqrt(msg + EPS) * gng_ref[...]
    gt_ref[...] = gn.reshape(GM_WIDTH, tm).astype(BF16)

    low = lax.broadcasted_iota(I32, (1, LANES), 1) < HEAD_DIM

    def two_head_norm(zt, g):
        sq = zt * zt
        s_lo = jnp.sum(jnp.where(low, sq, 0.0), axis=-1, keepdims=True)
        s_hi = jnp.sum(jnp.where(low, 0.0, sq), axis=-1, keepdims=True)
        r = jnp.where(low, lax.rsqrt(s_lo * (1.0 / HEAD_DIM) + EPS), lax.rsqrt(s_hi * (1.0 / HEAD_DIM) + EPS))
        return zt * r * g

    for j in range(ATTN_WIDTH // LANES):
        qn = two_head_norm(z[:, j * LANES:(j + 1) * LANES], qg_ref[...])
        q_ref[:, j * LANES:(j + 1) * LANES] = (qn * Q_SCALE).astype(BF16)
    for j in range(KV_WIDTH // LANES):
        c0 = ATTN_WIDTH + j * LANES
        k_ref[:, j * LANES:(j + 1) * LANES] = two_head_norm(z[:, c0:c0 + LANES], kg_ref[...]).astype(BF16)
    v_ref[...] = z[:, ATTN_WIDTH + KV_WIDTH:].astype(BF16)


def _in_proj(x2d, ang, wqkv, wugt, qg, kg, gng, cast=()):
    t = x2d.shape[0]
    tm = IN_TOKENS
    assert t % tm == 0
    steps = t // tm
    row = lambda i: (i, 0)
    col = lambda i: (0, i)
    cast_specs, cast_shapes = [], []
    for w in cast:
        assert w.shape[0] % steps == 0
        cast_specs.append(pl.BlockSpec((w.shape[0] // steps, w.shape[1]), row))
        cast_shapes.append(jax.ShapeDtypeStruct(w.shape, BF16))
    return pl.pallas_call(
        functools.partial(_in_proj_kernel, n_cast=len(cast)),
        grid=(steps,),
        in_specs=[
            pl.BlockSpec((tm, D_MODEL), row),
            _const_spec((1, D_MODEL)),
            _const_spec((D_MODEL, ATTN_WIDTH + 2 * KV_WIDTH)),
            _const_spec((2 * GM_WIDTH, D_MODEL)),
            _const_spec((1, LANES)),
            _const_spec((1, LANES)),
            _const_spec((GM_GROUPS, HEAD_DIM, 1)),
        ] + cast_specs,
        out_specs=[
            pl.BlockSpec((tm, ATTN_WIDTH), row),
            pl.BlockSpec((tm, KV_WIDTH), row),
            pl.BlockSpec((tm, KV_WIDTH), row),
            pl.BlockSpec((GM_WIDTH, tm), col),
            pl.BlockSpec((GM_WIDTH, tm), col),
        ] + cast_specs,
        out_shape=[
            jax.ShapeDtypeStruct((t, ATTN_WIDTH), BF16),
            jax.ShapeDtypeStruct((t, KV_WIDTH), BF16),
            jax.ShapeDtypeStruct((t, KV_WIDTH), BF16),
            jax.ShapeDtypeStruct((GM_WIDTH, t), F32),
            jax.ShapeDtypeStruct((GM_WIDTH, t), BF16),
        ] + cast_shapes,
        compiler_params=pltpu.CompilerParams(
            dimension_semantics=("arbitrary",), vmem_limit_bytes=VMEM_LIMIT),
        name="in_proj",
    )(x2d, ang, wqkv, wugt, qg, kg, gng, *cast)


def _mixer_kernel(q_ref, kp_ref, kc_ref, kn_ref, vp_ref, vc_ref, vn_ref, ut_ref, gt_ref, x_ref,
                  bias_ref, sink_ref, wsp_ref, bsp_ref, aog_ref, gog_ref, wout_ref, fng_ref,
                  wrs_ref, wrh_ref, x1_ref, xn_ref, aff_ref, *, blocks_per_seq):
    i = pl.program_id(0)
    tb = q_ref.shape[0]
    nsub = tb // CHUNK
    first = (i % blocks_per_seq) == 0
    last = (i % blocks_per_seq) == blocks_per_seq - 1

    kcat = jnp.concatenate([kp_ref[...], kc_ref[...], kn_ref[...]], axis=0)
    vcat = jnp.concatenate([vp_ref[...], vc_ref[...], vn_ref[...]], axis=0)
    key = lax.broadcasted_iota(I32, (1, 3 * CHUNK), 1)
    low = lax.broadcasted_iota(I32, (1, LANES), 1) < HEAD_DIM
    zero = jnp.zeros((), BF16)
    ones_bd = jnp.where((lax.broadcasted_iota(I32, (6 * CHUNK, LANES), 0) < 3 * CHUNK)
                        == (lax.broadcasted_iota(I32, (6 * CHUNK, LANES), 1) < HEAD_DIM), 1.0, 0.0).astype(BF16)

    def swap_halves(t):
        return jnp.concatenate([t[:, HEAD_DIM:], t[:, :HEAD_DIM]], axis=1)

    attn_rows = []
    for sb in range(nsub):
        edge = None
        if sb == 0:
            edge = jnp.where((key < CHUNK) & first, NEG_INF, 0.0)
        if sb == nsub - 1:
            hi = jnp.where((key >= 2 * CHUNK) & last, NEG_INF, 0.0)
            edge = hi if edge is None else edge + hi
        tiles = []
        for jt in range(N_KV_HEADS // 2):
            kt = kcat[sb * CHUNK:sb * CHUNK + 3 * CHUNK, jt * LANES:(jt + 1) * LANES]
            vt = vcat[sb * CHUNK:sb * CHUNK + 3 * CHUNK, jt * LANES:(jt + 1) * LANES]
            kts, vts = swap_halves(kt), swap_halves(vt)
            for jj in range(2):
                j = 2 * jt + jj
                ksame, kswap = (kt, kts) if jj == 0 else (kts, kt)
                vsame, vswap = (vt, vts) if jj == 0 else (vts, vt)
                krhs = jnp.concatenate([jnp.where(low, ksame, zero), jnp.where(low, zero, kswap)], axis=0)
                vrhs = jnp.concatenate([jnp.where(low, vsame, zero), jnp.where(low, zero, vswap)], axis=0)
                vaug = jnp.concatenate([vrhs, ones_bd], axis=1)
                for pr in range(GQA_GROUP // 2):
                    ha = GQA_GROUP * j + 2 * pr
                    qp = q_ref[sb * CHUNK:(sb + 1) * CHUNK, ha * HEAD_DIM:(ha + 2) * HEAD_DIM]
                    s = lax.dot_general(qp, krhs, _NT, preferred_element_type=F32)
                    ps, es = [], []
                    for g in range(2):
                        sh = s[:, g * 3 * CHUNK:(g + 1) * 3 * CHUNK] + bias_ref[ha + g]
                        if edge is not None:
                            sh = sh + edge
                        snk = sink_ref[ha + g]
                        m = jnp.maximum(jnp.max(sh, axis=-1, keepdims=True), snk)
                        ps.append(jnp.exp2(sh - m).astype(BF16))
                        es.append(jnp.exp2(snk - m))
                    o = jnp.dot(jnp.concatenate(ps, axis=1), vaug, preferred_element_type=F32)
                    den = o[:, LANES:] + jnp.where(low, es[0], es[1])
                    tiles.append(o[:, :LANES] / den)
        attn_rows.append(jnp.concatenate(tiles, axis=1))
    attn = jnp.concatenate(attn_rows, axis=0)
    an = attn * lax.rsqrt(jnp.mean(attn * attn, axis=-1, keepdims=True) + EPS) * aog_ref[...]

    st_rows = []
    for pi in range(GM_GROUPS // 2):
        r0 = pi * 2 * HEAD_DIM
        lhs = jnp.concatenate(
            [jnp.concatenate([gt_ref[r0:r0 + HEAD_DIM, c * CHUNK:(c + 1) * CHUNK],
                              gt_ref[r0 + HEAD_DIM:r0 + 2 * HEAD_DIM, c * CHUNK:(c + 1) * CHUNK]], axis=1)
             for c in range(nsub)], axis=0)
        o = jnp.dot(lhs, wsp_ref[pi], preferred_element_type=F32)
        ga = jnp.concatenate([o[c * HEAD_DIM:(c + 1) * HEAD_DIM, :CHUNK] for c in range(nsub)], axis=1)
        gb = jnp.concatenate([o[c * HEAD_DIM:(c + 1) * HEAD_DIM, CHUNK:] for c in range(nsub)], axis=1)
        ba = jnp.concatenate([bsp_ref[2 * pi]] * nsub, axis=1)
        bb = jnp.concatenate([bsp_ref[2 * pi + 1]] * nsub, axis=1)
        st_rows.append(ga + ba)
        st_rows.append(gb + bb)
    gmt = ut_ref[...] * jnp.concatenate(st_rows, axis=0)
    gm = gmt.T
    gn = gm * lax.rsqrt(jnp.mean(gm * gm, axis=-1, keepdims=True) + EPS) * gog_ref[...]

    cat = jnp.concatenate([an.astype(BF16), gn.astype(BF16)], axis=1)
    x1 = x_ref[...] + jnp.dot(cat, wout_ref[...], preferred_element_type=F32)
    x1_ref[...] = x1

    xn = x1 * lax.rsqrt(jnp.mean(x1 * x1, axis=-1, keepdims=True) + EPS) * fng_ref[...]
    xn_ref[...] = xn
    xh = xn.astype(BF16)
    xl = (xn - xh.astype(F32)).astype(BF16)
    r1 = lax.dot_general(wrs_ref[...], xh, _NT, preferred_element_type=F32)
    r2 = lax.dot_general(wrh_ref[...], xl, _NT, preferred_element_type=F32)
    logits = (r1[:N_EXPERTS] + r1[N_EXPERTS:]) + r2
    mx = jnp.max(logits, axis=0, keepdims=True)
    ex = jnp.exp(logits - mx)
    aff_ref[...] = ex / jnp.sum(ex, axis=0, keepdims=True)


def _mixer(q, k, v, ut, gt, x2d, seq_len, consts):
    t = q.shape[0]
    tb = MIX_TOKENS
    assert seq_len % tb == 0 and t % seq_len == 0
    r = tb // CHUNK
    nch = t // CHUNK
    row = lambda i: (i, 0)
    col = lambda i: (0, i)
    prev = lambda i: (jnp.maximum(i * r - 1, 0), 0)
    nxt = lambda i: (jnp.minimum((i + 1) * r, nch - 1), 0)
    (bias, sink, wsp, bsp, aog, gog, wout, fng, wrs, wrh) = consts
    kernel = functools.partial(_mixer_kernel, blocks_per_seq=seq_len // tb)
    return pl.pallas_call(
        kernel,
        grid=(t // tb,),
        in_specs=[
            pl.BlockSpec((tb, ATTN_WIDTH), row),
            pl.BlockSpec((CHUNK, KV_WIDTH), prev),
            pl.BlockSpec((tb, KV_WIDTH), row),
            pl.BlockSpec((CHUNK, KV_WIDTH), nxt),
            pl.BlockSpec((CHUNK, KV_WIDTH), prev),
            pl.BlockSpec((tb, KV_WIDTH), row),
            pl.BlockSpec((CHUNK, KV_WIDTH), nxt),
            pl.BlockSpec((GM_WIDTH, tb), col),
            pl.BlockSpec((GM_WIDTH, tb), col),
            pl.BlockSpec((tb, D_MODEL), row),
            _const_spec((N_Q_HEADS, CHUNK, 3 * CHUNK)),
            pl.BlockSpec(memory_space=pltpu.SMEM),
            _const_spec((GM_GROUPS // 2, 2 * CHUNK, 2 * CHUNK)),
            _const_spec((GM_GROUPS, 1, CHUNK)),
            _const_spec((1, ATTN_WIDTH)),
            _const_spec((1, GM_WIDTH)),
            _const_spec((D_MODEL, D_MODEL)),
            _const_spec((1, D_MODEL)),
            _const_spec((2 * N_EXPERTS, D_MODEL)),
            _const_spec((N_EXPERTS, D_MODEL)),
        ],
        out_specs=[
            pl.BlockSpec((tb, D_MODEL), row),
            pl.BlockSpec((tb, D_MODEL), row),
            pl.BlockSpec((N_EXPERTS, tb), col),
        ],
        out_shape=[
            jax.ShapeDtypeStruct((t, D_MODEL), F32),
            jax.ShapeDtypeStruct((t, D_MODEL), F32),
            jax.ShapeDtypeStruct((N_EXPERTS, t), F32),
        ],
        compiler_params=pltpu.CompilerParams(
            dimension_semantics=("arbitrary",), vmem_limit_bytes=VMEM_LIMIT),
        name="mixer",
    )(q, k, k, k, v, v, v, ut, gt, x2d, bias, sink, wsp, bsp, aog, gog, wout, fng, wrs, wrh)


def _sum12(x):
    return jnp.sum(jnp.sum(x, axis=2, keepdims=True), axis=1, keepdims=True)


def _select_kernel(a_ref, m_ref, mt_ref, at_ref, ppt_ref, ctok_ref, cnt_ref, *, cap, tok_bits):
    ne, nb, _ = a_ref.shape
    a = a_ref[...]
    capf = jnp.float32(cap)

    def thr_step(it, thr):
        cand = thr | jnp.left_shift(jnp.int32(1), 30 - it)
        cnt = _sum12(jnp.where(a >= lax.bitcast_convert_type(cand, F32), 1.0, 0.0))
        return jnp.where(cnt >= capf, cand, thr)

    thr = lax.bitcast_convert_type(lax.fori_loop(0, 31, thr_step, jnp.zeros((ne, 1, 1), I32)), F32)

    gt = a > thr
    eq = a == thr
    need = capf - _sum12(jnp.where(gt, 1.0, 0.0))
    tok = lax.broadcasted_iota(I32, (1, nb, LANES), 1) * LANES + lax.broadcasted_iota(I32, (1, nb, LANES), 2)

    def cut_step(it, c):
        cand = c + jnp.left_shift(jnp.int32(1), tok_bits - 1 - it)
        f = _sum12(jnp.where(eq & (tok < cand), 1.0, 0.0))
        return jnp.where(f < need, cand, c)

    c = lax.fori_loop(0, tok_bits, cut_step, jnp.zeros((ne, 1, 1), I32))
    tcut = jnp.where(need > 0.0, c + 1, 0)

    m = jnp.where(gt | (eq & (tok < tcut)), 1.0, 0.0)
    m_ref[...] = m.astype(BF16)

    tok_t = lax.broadcasted_iota(I32, (LANES, nb), 1) * LANES + lax.broadcasted_iota(I32, (LANES, nb), 0)
    mt = []
    for e in range(ne):
        a_te = a_ref[e].T
        at_ref[e] = a_te
        mt_e = jnp.where((a_te > thr[e]) | ((a_te == thr[e]) & (tok_t < tcut[e])), 1.0, 0.0)
        mt_ref[e] = mt_e.astype(BF16)
        mt.append(mt_e)

    cnt_t = functools.reduce(lambda x, y: x + y, mt)
    rl = lax.broadcasted_iota(I32, (LANES, LANES), 0)
    cl = lax.broadcasted_iota(I32, (LANES, LANES), 1)
    tri_strict_low = jnp.where(cl < rl, 1.0, 0.0).astype(BF16)
    local_t = jnp.dot(tri_strict_low, cnt_t.astype(BF16), preferred_element_type=F32)
    tot_t = jnp.sum(cnt_t, axis=0, keepdims=True)
    rb = lax.broadcasted_iota(I32, (nb, nb), 0)
    cb = lax.broadcasted_iota(I32, (nb, nb), 1)
    blk_before = jnp.where(rb < cb, 1.0, 0.0).astype(BF16)
    pb_t = _exact_dot(jnp.broadcast_to(tot_t, (8, nb)), blk_before)[0:1]
    ctok_t = local_t + pb_t
    rank = jnp.zeros((LANES, nb), F32)
    for e in range(ne):
        ppt_ref[e] = ctok_t + rank
        rank = rank + mt[e]

    cnt_b = jnp.sum(m, axis=0)
    tri_strict_up = jnp.where(rl < cl, 1.0, 0.0).astype(BF16)
    local_b = jnp.dot(cnt_b.astype(BF16), tri_strict_up, preferred_element_type=F32)
    totb = jnp.dot(cnt_b.astype(BF16), jnp.ones((LANES, LANES), BF16), preferred_element_type=F32)
    blk_after = jnp.where(cb < rb, 1.0, 0.0).astype(BF16)
    ctok_ref[...] = local_b + _exact_dot_left(blk_after, totb)
    cnt_ref[...] = cnt_b


def _select(a_blk, cap, tok_bits):
    ne, nb, _ = a_blk.shape
    kernel = functools.partial(_select_kernel, cap=cap, tok_bits=tok_bits)
    full3 = lambda s: pl.BlockSpec(s, lambda: (0, 0, 0))
    full2 = lambda s: pl.BlockSpec(s, lambda: (0, 0))
    return pl.pallas_call(
        kernel,
        in_specs=[full3((ne, nb, LANES))],
        out_specs=[full3((ne, nb, LANES)), full3((ne, LANES, nb)), full3((ne, LANES, nb)), full3((ne, LANES, nb)),
                   full2((nb, LANES)), full2((nb, LANES))],
        out_shape=[
            jax.ShapeDtypeStruct((ne, nb, LANES), BF16),
            jax.ShapeDtypeStruct((ne, LANES, nb), BF16),
            jax.ShapeDtypeStruct((ne, LANES, nb), F32),
            jax.ShapeDtypeStruct((ne, LANES, nb), F32),
            jax.ShapeDtypeStruct((nb, LANES), F32),
            jax.ShapeDtypeStruct((nb, LANES), F32),
        ],
        compiler_params=pltpu.CompilerParams(vmem_limit_bytes=VMEM_LIMIT),
        name="select",
    )(a_blk)


def _compact_kernel(m_ref, mt_ref, at_ref, ppt_ref, idx_ref, gate_ref, pp_ref):
    nb = m_ref.shape[1]
    nchunks = idx_ref.shape[1]
    mb = m_ref[0]
    mt = mt_ref[0]
    a_t = at_ref[0]
    pp_t = ppt_ref[0]

    totb = jnp.dot(mb, jnp.ones((LANES, LANES), BF16), preferred_element_type=F32)
    rb = lax.broadcasted_iota(I32, (nb, nb), 0)
    cb = lax.broadcasted_iota(I32, (nb, nb), 1)
    blk_after = jnp.where(cb < rb, 1.0, 0.0).astype(BF16)
    cb_excl = jnp.dot(blk_after, totb.astype(BF16), preferred_element_type=F32)
    cb_incl = cb_excl + totb
    rl = lax.broadcasted_iota(I32, (LANES, LANES), 0)
    cl = lax.broadcasted_iota(I32, (LANES, LANES), 1)
    tri_low = jnp.where(cl <= rl, 1.0, 0.0).astype(BF16)
    lt = jnp.dot(tri_low, mt, preferred_element_type=F32).astype(BF16)
    sub_b = lax.broadcasted_iota(I32, (nb, LANES), 0).astype(F32)
    sub_l = rl.astype(F32)
    lane = lax.broadcasted_iota(I32, (1, LANES), 1).astype(F32)

    def chunk(c, carry):
        j = lane + lax.convert_element_type(c * LANES, F32)
        bj = jnp.sum(jnp.where(cb_incl <= j, 1.0, 0.0), axis=0, keepdims=True)
        oh = sub_b == bj
        ohb = jnp.where(oh, 1.0, 0.0).astype(BF16)
        jloc = j - jnp.sum(jnp.where(oh, cb_excl, 0.0), axis=0, keepdims=True)
        glt = jnp.dot(lt, ohb, preferred_element_type=F32)
        lj = jnp.sum(jnp.where(glt <= jloc, 1.0, 0.0), axis=0, keepdims=True)
        ohl = sub_l == lj
        gate = jnp.sum(jnp.where(ohl, _exact_dot(a_t, ohb), 0.0), axis=0, keepdims=True)
        pp = jnp.sum(jnp.where(ohl, _exact_dot(pp_t, ohb), 0.0), axis=0, keepdims=True)
        idx_ref[0, pl.ds(c, 1), :] = (bj * LANES + lj).astype(I32)
        gate_ref[0, pl.ds(c, 1), :] = gate
        pp_ref[0, pl.ds(c, 1), :] = pp.astype(I32)
        return carry

    lax.fori_loop(0, nchunks, chunk, 0, unroll=min(4, nchunks))


def _compact(m_blk, m_t, a_t, pp_t, cap):
    ne, nb, _ = m_blk.shape
    assert cap % LANES == 0
    nchunks = cap // LANES
    per_e = lambda s: pl.BlockSpec((1,) + s, lambda e: (e, 0, 0))
    return pl.pallas_call(
        _compact_kernel,
        grid=(ne,),
        in_specs=[per_e((nb, LANES)), per_e((LANES, nb)), per_e((LANES, nb)), per_e((LANES, nb))],
        out_specs=[per_e((nchunks, LANES))] * 3,
        out_shape=[
            jax.ShapeDtypeStruct((ne, nchunks, LANES), I32),
            jax.ShapeDtypeStruct((ne, nchunks, LANES), F32),
            jax.ShapeDtypeStruct((ne, nchunks, LANES), I32),
        ],
        compiler_params=pltpu.CompilerParams(
            dimension_semantics=("arbitrary",), vmem_limit_bytes=VMEM_LIMIT),
        name="compact",
    )(m_blk, m_t, a_t, pp_t)


def _ffn_kernel(idx_ref, idxn_ref, ppp_ref, ppc_ref, gate_ref, x_hbm, wg_ref, wu_ref, wd_ref,
                g_hbm, buf, gsem, ssem, *, tiles_per_expert):
    tm = buf.shape[1]
    step = pl.program_id(0) * tiles_per_expert + pl.program_id(1)
    nsteps = pl.num_programs(0) * tiles_per_expert
    slot = step % 2
    other = 1 - slot

    def gather_copy(row, r, s):
        return pltpu.make_async_copy(x_hbm.at[pl.ds(row, 1), :], buf.at[s, pl.ds(r, 1), :], gsem.at[s])

    def scatter_copy(row, r, s):
        return pltpu.make_async_copy(buf.at[2 + s, pl.ds(r, 1), :], g_hbm.at[pl.ds(row, 1), :], ssem.at[s])

    def wait_gather(s):
        pltpu.make_async_copy(x_hbm.at[pl.ds(0, tm), :], buf.at[s], gsem.at[s]).wait()

    def wait_scatter(s):
        pltpu.make_async_copy(buf.at[2 + s], g_hbm.at[pl.ds(0, tm), :], ssem.at[s]).wait()

    @pl.when(step == 0)
    def _():
        def body(r, carry):
            gather_copy(idx_ref[0, 0, r], r, 0).start()
            return carry
        lax.fori_loop(0, tm, body, 0, unroll=8)
        buf[3] = jnp.zeros((tm, D_MODEL), F32)

    def tile(ps):
        po = 1 - ps

        @pl.when(step > 0)
        def _():
            wait_scatter(ps)

        wait_gather(ps)

        yslot = 2 + slot

        gw = LANES if tm % LANES == 0 else tm
        ri = lax.broadcasted_iota(I32, (gw, gw), 0)
        ci = lax.broadcasted_iota(I32, (gw, gw), 1)
        eye = jnp.where(ri == ci, 1.0, 0.0).astype(BF16)
        g1, g2, g3 = _split3(jnp.broadcast_to(gate_ref[0], (8, tm)))
        nt = lambda g: lax.dot_general(eye, g, _NT, preferred_element_type=F32)
        gcol = jnp.concatenate(
            [((nt(g1[:, b:b + gw]) + nt(g2[:, b:b + gw])) + nt(g3[:, b:b + gw]))[:, 0:1] for b in range(0, tm, gw)],
            axis=0)

        nchunk = EXPERT_FF // FFN_CHUNK
        nhalf = FFN_ROW_BLOCKS if tm % (16 * FFN_ROW_BLOCKS) == 0 else 1
        hr = tm // nhalf
        ngroup = max(nchunk - 1, 1)
        bounds = [tm * k // ngroup for k in range(ngroup + 1)] + [tm] * (nchunk - ngroup)

        xns = [buf[slot, h * hr:(h + 1) * hr, :].astype(BF16) for h in range(nhalf)]

        def gate_up(h, c):
            cols = slice(c * FFN_CHUNK, (c + 1) * FFN_CHUNK)
            return (jnp.dot(xns[h], wg_ref[0, :, cols], preferred_element_type=F32),
                    jnp.dot(xns[h], wu_ref[0, :, cols], preferred_element_type=F32))

        nxt = [gate_up(h, 0) for h in range(nhalf)]
        for c in range(nchunk):
            for r in range(bounds[c], bounds[c + 1]):
                gather_copy(idxn_ref[0, 0, r], r, po).start()
                scatter_copy(ppp_ref[0, 0, r], r, po).start()
            for h in range(nhalf):
                rows = slice(h * hr, (h + 1) * hr)
                hg, hu = nxt[h]
                if c + 1 < nchunk:
                    nxt[h] = gate_up(h, c + 1)
                hh = (jax.nn.silu(hg) * hu * gcol[rows]).astype(BF16)
                part = jnp.dot(hh, wd_ref[0, c * FFN_CHUNK:(c + 1) * FFN_CHUNK, :], preferred_element_type=F32)
                if c == 0:
                    buf[yslot, rows, :] = part
                else:
                    buf[yslot, rows, :] = buf[yslot, rows, :] + part

    for ps in range(2):
        pl.when(slot == ps)(functools.partial(tile, ps))

    @pl.when(step + 1 == nsteps)
    def _():
        def body(r, carry):
            scatter_copy(ppc_ref[0, 0, r], r, slot).start()
            return carry
        lax.fori_loop(0, tm, body, 0, unroll=8)
        wait_scatter(other)
        wait_scatter(slot)
        wait_gather(other)


def _ffn(idx_t, gate_t, pp_t, xn, wg, wu, wd, n_pairs):
    ntiles, _, tm = idx_t.shape
    tpe = ntiles // N_EXPERTS
    assert tm % (EXPERT_FF // FFN_CHUNK) == 0
    cur = lambda e, i: (e * tpe + i, 0, 0)
    nxt = lambda e, i: (jnp.minimum(e * tpe + i + 1, ntiles - 1), 0, 0)
    prv = lambda e, i: (e * tpe + i, 0, 0)
    cur1 = lambda e, i: (e * tpe + i + 1, 0, 0)
    wmap = lambda e, i: (e, 0, 0)
    spare = (n_pairs + jnp.arange(tm, dtype=I32)).reshape(1, 1, tm)
    pp_ext = jnp.concatenate([spare, pp_t], axis=0)
    kernel = functools.partial(_ffn_kernel, tiles_per_expert=tpe)
    return pl.pallas_call(
        kernel,
        grid=(N_EXPERTS, tpe),
        in_specs=[
            pl.BlockSpec((1, 1, tm), cur, memory_space=pltpu.SMEM),
            pl.BlockSpec((1, 1, tm), nxt, memory_space=pltpu.SMEM),
            pl.BlockSpec((1, 1, tm), prv, memory_space=pltpu.SMEM),
            pl.BlockSpec((1, 1, tm), cur1, memory_space=pltpu.SMEM),
            pl.BlockSpec((1, 1, tm), cur),
            pl.BlockSpec(memory_space=pl.ANY),
            pl.BlockSpec((1, D_MODEL, EXPERT_FF), wmap),
            pl.BlockSpec((1, D_MODEL, EXPERT_FF), wmap),
            pl.BlockSpec((1, EXPERT_FF, D_MODEL), wmap),
        ],
        out_specs=pl.BlockSpec(memory_space=pl.ANY),
        out_shape=jax.ShapeDtypeStruct((n_pairs + tm, D_MODEL), F32),
        scratch_shapes=[
            pltpu.VMEM((4, tm, D_MODEL), F32),
            pltpu.SemaphoreType.DMA((2,)),
            pltpu.SemaphoreType.DMA((2,)),
        ],
        compiler_params=pltpu.CompilerParams(
            dimension_semantics=("arbitrary", "arbitrary"), vmem_limit_bytes=VMEM_LIMIT),
        name="ffn",
    )(idx_t, idx_t, pp_ext, pp_ext, gate_t, xn, wg, wu, wd)


def _combine_kernel(boff_ref, x1_ref, ctok_ref, cnt_ref, g_hbm, y_ref, gbuf, acc_ref, sem, st_ref, *, n_pairs):
    i = pl.program_id(0)
    tb = x1_ref.shape[0]
    nslot, pc, _ = gbuf.shape
    n_chunks = n_pairs // pc
    r = tb // LANES

    def chunk_copy(g):
        s = g % nslot
        return pltpu.make_async_copy(g_hbm.at[pl.ds(g * pc, pc), :], gbuf.at[s], sem.at[s])

    @pl.when(i == 0)
    def _():
        st_ref[0] = 0
        st_ref[1] = 0

    g_lo = boff_ref[i] // pc
    g_hi = (boff_ref[i + 1] + pc - 1) // pc

    ri = lax.broadcasted_iota(I32, (LANES, LANES), 0)
    ci = lax.broadcasted_iota(I32, (LANES, LANES), 1)
    eye = jnp.where(ri == ci, 1.0, 0.0).astype(BF16)

    def to_col(rowv):
        p1, p2, p3 = _split3(jnp.broadcast_to(rowv, (8, LANES)))
        nt = lambda p: lax.dot_general(eye, p, _NT, preferred_element_type=F32)
        return ((nt(p1) + nt(p2)) + nt(p3))[:, 0:1]

    cex_rows = ctok_ref[pl.ds(i * r, r), :]
    cnt_rows = cnt_ref[pl.ds(i * r, r), :]
    cex = jnp.concatenate([to_col(cex_rows[k:k + 1]) for k in range(r)], axis=0)
    cin = cex + jnp.concatenate([to_col(cnt_rows[k:k + 1]) for k in range(r)], axis=0)

    acc_ref[...] = jnp.zeros_like(acc_ref)

    def body(g, carry):
        def start_one(k, c2):
            chunk_copy(k).start()
            return c2
        upto = jnp.minimum(g + nslot, n_chunks)
        lax.fori_loop(st_ref[0], upto, start_one, 0)
        st_ref[0] = jnp.maximum(st_ref[0], upto)

        @pl.when(g >= st_ref[1])
        def _():
            chunk_copy(g).wait()
            st_ref[1] = g + 1

        p = (g * pc + lax.broadcasted_iota(I32, (1, pc), 1)).astype(F32)
        sel = (cex <= p) & (p < cin)
        acc_ref[...] += jnp.dot(jnp.where(sel, 1.0, 0.0).astype(BF16), gbuf[g % nslot].astype(BF16),
                                preferred_element_type=F32)
        return carry

    lax.fori_loop(g_lo, g_hi, body, 0)
    y_ref[...] = x1_ref[...] + acc_ref[...]


def _combine(boff, x1, ctok, cnt, g, n_pairs):
    t = x1.shape[0]
    tb = COMB_TOKENS
    nb = ctok.shape[0]
    assert t % tb == 0 and n_pairs % COMB_PAIRS == 0
    kernel = functools.partial(_combine_kernel, n_pairs=n_pairs)
    grid_spec = pltpu.PrefetchScalarGridSpec(
        num_scalar_prefetch=1,
        grid=(t // tb,),
        in_specs=[
            pl.BlockSpec((tb, D_MODEL), lambda i, b: (i, 0)),
            pl.BlockSpec((nb, LANES), lambda i, b: (0, 0)),
            pl.BlockSpec((nb, LANES), lambda i, b: (0, 0)),
            pl.BlockSpec(memory_space=pl.ANY),
        ],
        out_specs=pl.BlockSpec((tb, D_MODEL), lambda i, b: (i, 0)),
        scratch_shapes=[
            pltpu.VMEM((COMB_SLOTS, COMB_PAIRS, D_MODEL), F32),
            pltpu.VMEM((tb, D_MODEL), F32),
            pltpu.SemaphoreType.DMA((COMB_SLOTS,)),
            pltpu.SMEM((2,), I32),
        ],
    )
    return pl.pallas_call(
        kernel,
        grid_spec=grid_spec,
        out_shape=jax.ShapeDtypeStruct((t, D_MODEL), F32),
        compiler_params=pltpu.CompilerParams(
            dimension_semantics=("arbitrary",), vmem_limit_bytes=VMEM_LIMIT),
        name="combine",
    )(boff, x1, ctok, cnt, g)


def _alibi_bias():
    slopes = 2.0 ** (-8.0 * jnp.arange(1, N_Q_HEADS + 1, dtype=F32) / N_Q_HEADS)
    qi = jnp.arange(CHUNK)[:, None]
    kj = jnp.arange(3 * CHUNK)[None, :]
    rel = jnp.abs(qi + CHUNK - kj)
    bias = -slopes[:, None, None] * rel.astype(F32)[None] * LOG2E
    return jnp.where((rel <= CHUNK)[None], bias, NEG_INF)


def _prep_weights(attn_norm_g, w_in, q_norm_g, k_norm_g, sink, gm_norm_g, w_spatial, b_spatial,
                  attn_out_g, gm_out_g, w_out, ffn_norm_g, w_router, w_gate, w_up, w_down):
    nqkv = ATTN_WIDTH + 2 * KV_WIDTH
    ang = attn_norm_g.reshape(1, D_MODEL)
    wqkv = w_in[:, :nqkv].astype(BF16)
    wugt = w_in[:, nqkv:].T.astype(BF16)
    qg = jnp.tile(q_norm_g, 2).reshape(1, LANES)
    kg = jnp.tile(k_norm_g, 2).reshape(1, LANES)
    gng = gm_norm_g.reshape(GM_GROUPS, HEAD_DIM, 1)
    wst = jnp.swapaxes(w_spatial, 1, 2).astype(BF16)
    zero = jnp.zeros((GM_GROUPS // 2, CHUNK, CHUNK), BF16)
    wsp = jnp.concatenate([jnp.concatenate([wst[0::2], zero], axis=2),
                           jnp.concatenate([zero, wst[1::2]], axis=2)], axis=1)
    bsp = b_spatial.reshape(GM_GROUPS, 1, CHUNK)
    aog = attn_out_g.reshape(1, ATTN_WIDTH)
    gog = gm_out_g.reshape(1, GM_WIDTH)
    wout = w_out.astype(BF16)
    fng = ffn_norm_g.reshape(1, D_MODEL)
    wr_t = w_router.T
    wr_hi = wr_t.astype(BF16)
    wr_lo = (wr_t - wr_hi.astype(F32)).astype(BF16)
    wrs = jnp.concatenate([wr_hi, wr_lo], axis=0)
    mixer_consts = (_alibi_bias(), sink * LOG2E, wsp, bsp, aog, gog, wout, fng, wrs, wr_hi)
    in_consts = (ang, wqkv, wugt, qg, kg, gng)
    return in_consts, mixer_consts


def _cast_plan(weights, steps):
    views = [w.reshape(-1, w.shape[-1]) for w in weights]
    ok = all(v.shape[0] % steps == 0 and (v.shape[0] // steps) * v.shape[1] * 4 <= CAST_SLAB_BYTES for v in views)
    return views if ok else None


def _front(x, in_consts, mixer_consts, cast_weights):
    b, s, d = x.shape
    t = b * s
    x2d = x.reshape(t, d)
    views = _cast_plan(cast_weights, t // IN_TOKENS)
    outs = _in_proj(x2d, *in_consts, cast=tuple(views) if views is not None else ())
    q, k, v, ut, gt = outs[:5]
    if views is not None:
        cast_done = [o.reshape(w.shape) for o, w in zip(outs[5:], cast_weights)]
    else:
        cast_done = [w.astype(BF16) for w in cast_weights]
    x1, xn, aff_t = _mixer(q, k, v, ut, gt, x2d, s, mixer_consts)

    nb = t // LANES
    cap = max(1, CAPACITY_FACTOR * t // N_EXPERTS)
    tok_bits = max(1, (t - 1).bit_length())
    a_blk = aff_t.reshape(N_EXPERTS, nb, LANES)
    m_blk, m_t, a_t, pp_t, ctok, cnt = _select(a_blk, cap, tok_bits)
    idx, gate, pp = _compact(m_blk, m_t, a_t, pp_t, cap)
    return (x1, xn, idx, gate, pp, ctok, cnt, cap), cast_done


def _back(front, wg, wu, wd, shape):
    x1, xn, idx, gate, pp, ctok, cnt, cap = front
    tm = min(FFN_ROWS, cap)
    ntiles = N_EXPERTS * cap // tm
    n_pairs = N_EXPERTS * cap
    g = _ffn(idx.reshape(ntiles, 1, tm), gate.reshape(ntiles, 1, tm), pp.reshape(ntiles, 1, tm),
             xn, wg, wu, wd, n_pairs)
    boff = jnp.concatenate([ctok.reshape(-1)[::COMB_TOKENS].astype(I32), jnp.full((1,), n_pairs, I32)])
    return _combine(boff, x1, ctok, cnt, g, n_pairs).reshape(shape)


def kernel(x_prompt, x_sample, attn_norm_g, w_in, q_norm_g, k_norm_g, sink, gm_norm_g, w_spatial, b_spatial, attn_out_g, gm_out_g, w_out, ffn_norm_g, w_router, w_gate, w_up, w_down):
    y_prompt, y_sample = x_prompt, x_sample
    for l in range(w_in.shape[0]):
        in_consts, mixer_consts = _prep_weights(
            attn_norm_g[l], w_in[l], q_norm_g[l], k_norm_g[l], sink[l], gm_norm_g[l], w_spatial[l], b_spatial[l],
            attn_out_g[l], gm_out_g[l], w_out[l], ffn_norm_g[l], w_router[l], w_gate[l], w_up[l], w_down[l])
        front_p, (wg,) = _front(y_prompt, in_consts, mixer_consts, [w_gate[l]])
        front_s, (wu, wd) = _front(y_sample, in_consts, mixer_consts, [w_up[l], w_down[l]])
        y_prompt = _back(front_p, wg, wu, wd, y_prompt.shape)
        y_sample = _back(front_s, wg, wu, wd, y_sample.shape)
    return (y_prompt, y_sample)
```
